```python
import jax, jax.numpy as jnp
from jax import lax
import numpy as np

D_MODEL = 1024
BATCH = 8
SEQ = 4096
DEPTH = 2

GRID_W = 64
CTX_LEN = 256
H_M = 4
DH_M = 128
M_W = H_M * DH_M
H_A = 8
H_KV = 2
DH_A = 64
A_Q = H_A * DH_A
A_KV = H_KV * DH_A
GQA_GROUP = H_A // H_KV
WINDOW = 128
WIN_BLOCK = 128
ROPE_THETA = 10000.0
AB_WIDTHS = (M_W, M_W, M_W, M_W, 4 * H_M, A_Q, A_KV, A_KV)
MIX_W = M_W + A_Q
H_C = 4
DK_C = 128
DV_C = 256
C_K = H_C * DK_C
C_V = H_C * DV_C
GATE_RANK = 16
GATE_TAU = 16.0
C_WIDTHS = (C_K, C_K, C_V, C_V, 2 * GATE_RANK)
CHUNK = 64
N_KEYS = 128
N_EXPERTS = N_KEYS * N_KEYS
PEER_HEADS = 8
PEER_TOPK = 16
PEER_KEY_DIM = 256
PEER_HALF = PEER_KEY_DIM // 2
PEER_CHUNK = 128
N_EVEN = (DEPTH + 1) // 2
N_ODD = DEPTH // 2
DEEPNORM_ALPHA = (2 * DEPTH) ** 0.25
DEEPNORM_BETA = (8 * DEPTH) ** -0.25
LN_EPS = 1e-5

kernel_name = "hybrid_mlstm_swa_gla_peer_diffusion_block"


def _layernorm(x, w, b):
    xf = x.astype(jnp.float32)
    mu = xf.mean(-1, keepdims=True)
    var = jnp.mean(jnp.square(xf - mu), -1, keepdims=True)
    return ((xf - mu) * lax.rsqrt(var + LN_EPS)).astype(x.dtype) * w + b


def _split(a, widths):
    idx = np.cumsum(widths)[:-1].tolist()
    return jnp.split(a, idx, axis=-1)


def _heads(a, h):
    b, s = a.shape[:2]
    return a.reshape(b, s, h, -1).transpose(0, 2, 1, 3)


def _head_norm(h, w):
    hf = h.astype(jnp.float32)
    hf = hf * lax.rsqrt(jnp.mean(hf * hf, -1, keepdims=True) + LN_EPS)
    b, nh, s, d = h.shape
    return hf.transpose(0, 2, 1, 3).reshape(b, s, nh * d) * w


def _to_chunks(a):
    b, h, s = a.shape[:3]
    a = a.reshape((b, h, s // CHUNK, CHUNK) + a.shape[3:])
    return jnp.moveaxis(a, 2, 0)


def _from_chunks(a):
    a = jnp.moveaxis(a, 0, 2)
    return a.reshape(a.shape[:2] + (-1,) + a.shape[4:])


def _mlstm_scan(q, k, v, log_i, log_f, state):
    causal = jnp.tril(jnp.ones((CHUNK, CHUNK), bool))

    def step(carry, inp):
        c_st, n_st, m_st = carry
        qc, kc, vc, li, lf = inp
        cum = jnp.cumsum(lf, axis=-1)
        dmat = cum[..., :, None] - cum[..., None, :] + li[..., None, :]
        dmat = jnp.where(causal, dmat, -jnp.inf)
        m_inter = cum + m_st[..., None]
        m_t = jnp.maximum(m_inter, dmat.max(-1))
        w = jnp.exp(dmat - m_t[..., None]) * jnp.einsum('bhtd,bhsd->bhts', qc, kc)
        a_inter = jnp.exp(m_inter - m_t)
        num = a_inter[..., None] * jnp.einsum('bhtd,bhde->bhte', qc, c_st) + jnp.einsum('bhts,bhse->bhte', w, vc)
        den = a_inter * jnp.einsum('bhtd,bhd->bht', qc, n_st) + w.sum(-1)
        h = num / jnp.maximum(jnp.abs(den), jnp.exp(-m_t))[..., None]
        total = cum[..., -1]
        decay_s = total[..., None] - cum + li
        m_new = jnp.maximum(total + m_st, decay_s.max(-1))
        ws = jnp.exp(decay_s - m_new[..., None])
        a_st = jnp.exp(total + m_st - m_new)
        c_st = a_st[..., None, None] * c_st + jnp.einsum('bhs,bhsd,bhse->bhde', ws, kc, vc)
        n_st = a_st[..., None] * n_st + jnp.einsum('bhs,bhsd->bhd', ws, kc)
        return (c_st, n_st, m_new), h

    state, hs = lax.scan(step, state, tuple(_to_chunks(a) for a in (q, k, v, log_i, log_f)))
    return _from_chunks(hs), state


def _gla_scan(q, k, v, log_a, state):
    causal = jnp.tril(jnp.ones((CHUNK, CHUNK), bool))

    def step(st, inp):
        qc, kc, vc, la = inp
        cum = jnp.cumsum(la, axis=2)
        rel = cum[:, :, :, None, :] - cum[:, :, None, :, :]
        rel = jnp.where(causal[:, :, None], rel, -jnp.inf)
        scores = jnp.einsum('bhtk,bhsk,bhtsk->bhts', qc, kc, jnp.exp(rel))
        out = jnp.einsum('bhtk,bhkv->bhtv', qc * jnp.exp(cum), st) + jnp.einsum('bhts,bhsv->bhtv', scores, vc)
        total = cum[:, :, -1:, :]
        st = jnp.exp(total[:, :, 0, :, None]) * st + jnp.einsum('bhsk,bhsv->bhkv', kc * jnp.exp(total - cum), vc)
        return st, out

    state, outs = lax.scan(step, state, tuple(_to_chunks(a) for a in (q, k, v, log_a)))
    return _from_chunks(outs), state


def _directional(scan_fn, ctx_seq, lat_seq, init, reverse):
    if reverse:
        ctx_seq = tuple(jnp.flip(a, 2) for a in ctx_seq)
        lat_seq = tuple(jnp.flip(a, 2) for a in lat_seq)
    h_ctx, state = scan_fn(*ctx_seq, init)
    h_lat, _ = scan_fn(*lat_seq, state)
    if reverse:
        h_ctx, h_lat = jnp.flip(h_ctx, 2), jnp.flip(h_lat, 2)
    return h_ctx, h_lat


def _axial_rope(s):
    rows = s // GRID_W
    row = jnp.repeat(jnp.arange(rows), GRID_W).astype(jnp.float32)
    col = jnp.tile(jnp.arange(GRID_W), rows).astype(jnp.float32)
    n_freq = DH_A // 4
    inv = ROPE_THETA ** (-jnp.arange(n_freq, dtype=jnp.float32) / n_freq)
    ang = jnp.concatenate([row[:, None] * inv, col[:, None] * inv], -1)
    return jnp.cos(ang), jnp.sin(ang)


def _rope(x, cos, sin):
    xf = x.astype(jnp.float32)
    x1, x2 = xf[..., 0::2], xf[..., 1::2]
    c = cos[None, :, None, :]
    sn = sin[None, :, None, :]
    return jnp.stack([x1 * c - x2 * sn, x1 * sn + x2 * c], -1).reshape(x.shape).astype(x.dtype)


def _window_attention(q, k, v, k_ctx, v_ctx, sink):
    b, s = q.shape[:2]
    nb = s // WIN_BLOCK
    lc = k_ctx.shape[1]
    qb = jnp.moveaxis(q.reshape(b, nb, WIN_BLOCK, H_KV, GQA_GROUP, DH_A), 1, 0)

    def bands(a):
        ap = jnp.pad(a, ((0, 0), (WIN_BLOCK, WIN_BLOCK), (0, 0), (0, 0))).reshape(b, nb + 2, WIN_BLOCK, H_KV, DH_A)
        return jnp.moveaxis(jnp.concatenate([ap[:, :-2], ap[:, 1:-1], ap[:, 2:]], 2), 1, 0)

    kw, vw = bands(k), bands(v)
    qi = jnp.arange(WIN_BLOCK)[:, None]
    kj = jnp.arange(3 * WIN_BLOCK)[None, :]
    key_pos = jnp.arange(nb)[:, None, None] * WIN_BLOCK + kj - WIN_BLOCK
    valid = (jnp.abs(kj - WIN_BLOCK - qi) <= WINDOW) & (key_pos >= 0) & (key_pos < s)
    scale = DH_A ** -0.5
    sink_l = sink.reshape(H_KV, GQA_GROUP).astype(jnp.float32)

    def block(args):
        qn, kn, vn, mask = args
        s_ctx = jnp.einsum('bqhgd,bchd->bhgqc', qn, k_ctx).astype(jnp.float32) * scale
        s_win = jnp.einsum('bqhgd,bkhd->bhgqk', qn, kn).astype(jnp.float32) * scale
        s_win = jnp.where(mask, s_win, -jnp.inf)
        sink_col = jnp.broadcast_to(sink_l[None, :, :, None, None], s_ctx.shape[:-1] + (1,))
        p = jax.nn.softmax(jnp.concatenate([sink_col, s_ctx, s_win], -1), -1).astype(qn.dtype)
        return (jnp.einsum('bhgqc,bchd->bqhgd', p[..., 1:1 + lc], v_ctx)
                + jnp.einsum('bhgqk,bkhd->bqhgd', p[..., 1 + lc:], vn))

    out = lax.map(block, (qb, kw, vw, valid))
    return jnp.moveaxis(out, 0, 1).reshape(b, s, H_A, DH_A)


def _context_attention(q, k, v, sink):
    b, lc = q.shape[:2]
    qg = q.reshape(b, lc, H_KV, GQA_GROUP, DH_A)
    sc = jnp.einsum('bqhgd,bkhd->bhgqk', qg, k).astype(jnp.float32) * DH_A ** -0.5
    sink_col = jnp.broadcast_to(sink.reshape(H_KV, GQA_GROUP).astype(jnp.float32)[None, :, :, None, None],
                                sc.shape[:-1] + (1,))
    p = jax.nn.softmax(jnp.concatenate([sink_col, sc], -1), -1)[..., 1:].astype(q.dtype)
    return jnp.einsum('bhgqk,bkhd->bqhgd', p, v).reshape(b, lc, H_A, DH_A)


def _ab_streams(h, w_in, gate_b):
    q_m, k_m, v_m, o_m, g_m, q_a, k_a, v_a = _split(h @ w_in, AB_WIDTHS)
    b, s = h.shape[:2]
    g = g_m.reshape(b, s, 2, 2, H_M).astype(jnp.float32) + gate_b.astype(jnp.float32)
    mlstm = (_heads(q_m, H_M).astype(jnp.float32),
             _heads(k_m, H_M).astype(jnp.float32) * DH_M ** -0.5,
             _heads(v_m, H_M).astype(jnp.float32))
    gates = [(g[:, :, d, 0].transpose(0, 2, 1), jax.nn.log_sigmoid(g[:, :, d, 1]).transpose(0, 2, 1))
             for d in range(2)]
    attn = (q_a.reshape(b, s, H_A, DH_A), k_a.reshape(b, s, H_KV, DH_A), v_a.reshape(b, s, H_KV, DH_A))
    return mlstm, gates, o_m, attn


def _merge_ab(m, o, a, norm_w, w_out):
    hm = _head_norm(m, norm_w).astype(o.dtype) * jax.nn.sigmoid(o)
    cat = jnp.concatenate([hm, a.reshape(a.shape[0], a.shape[1], A_Q)], -1)
    return cat @ w_out


def _mixer_ab(hl, hc, w_in, gate_b, norm_w, sink, w_out, ctx_out):
    ml, gl, ol, (ql, kl, vl) = _ab_streams(hl, w_in, gate_b)
    mc, gc, oc, (qc, kc, vc) = _ab_streams(hc, w_in, gate_b)
    b = hl.shape[0]
    init = (jnp.zeros((b, H_M, DH_M, DH_M), jnp.float32), jnp.zeros((b, H_M, DH_M), jnp.float32),
            jnp.zeros((b, H_M), jnp.float32))
    outs = [_directional(_mlstm_scan, mc + gc[d], ml + gl[d], init, d == 1) for d in range(2)]
    cos, sin = _axial_rope(hl.shape[1])
    a_lat = _window_attention(_rope(ql, cos, sin), _rope(kl, cos, sin), vl, kc, vc, sink)
    y_lat = _merge_ab(outs[0][1] + outs[1][1], ol, a_lat, norm_w, w_out)
    if not ctx_out:
        return y_lat, None
    y_ctx = _merge_ab(outs[0][0] + outs[1][0], oc, _context_attention(qc, kc, vc, sink), norm_w, w_out)
    return y_lat, y_ctx


def _c_streams(h, w_in, gate_up, gate_b):
    q, k, v, g, low = _split(h @ w_in, C_WIDTHS)
    b, s = h.shape[:2]
    qkv = (_heads(q, H_C).astype(jnp.float32) * DK_C ** -0.5,
           _heads(k, H_C).astype(jnp.float32),
           _heads(v, H_C).astype(jnp.float32))
    low = low.reshape(b, s, 2, GATE_RANK)
    log_a = [_heads(jax.nn.log_sigmoid((low[:, :, d] @ gate_up[d] + gate_b[d]).astype(jnp.float32)) / GATE_TAU, H_C)
             for d in range(2)]
    return qkv, log_a, g


def _merge_c(h, g, norm_w, w_out):
    return (_head_norm(h, norm_w).astype(g.dtype) * jax.nn.silu(g)) @ w_out


def _mixer_c(hl, hc, w_in, gate_up, gate_b, norm_w, w_out, ctx_out):
    sl, al, gl = _c_streams(hl, w_in, gate_up, gate_b)
    sc, ac, gc = _c_streams(hc, w_in, gate_up, gate_b)
    init = jnp.zeros((hl.shape[0], H_C, DK_C, DV_C), jnp.float32)
    outs = [_directional(_gla_scan, sc + (ac[d],), sl + (al[d],), init, d == 1) for d in range(2)]
    y_lat = _merge_c(outs[0][1] + outs[1][1], gl, norm_w, w_out)
    if not ctx_out:
        return y_lat, None
    return y_lat, _merge_c(outs[0][0] + outs[1][0], gc, norm_w, w_out)


def _peer(h, wq, keys, u, v):
    shape = h.shape
    tokens = h.reshape(-1, PEER_CHUNK, shape[-1])

    def block(hb):
        tc = hb.shape[0]
        q = (hb @ wq).reshape(tc, PEER_HEADS, 2, PEER_HALF)
        s = jnp.einsum('thpk,hpnk->thpn', q, keys)
        s_top, i_top = lax.top_k(s, PEER_TOPK)
        cand = s_top[:, :, 0, :, None] + s_top[:, :, 1, None, :]
        cid = i_top[:, :, 0, :, None] * N_KEYS + i_top[:, :, 1, None, :]
        best, pos = lax.top_k(cand.reshape(tc, PEER_HEADS, -1), PEER_TOPK)
        eid = jnp.take_along_axis(cid.reshape(tc, PEER_HEADS, -1), pos, -1)
        g = jax.nn.softmax(best.astype(jnp.float32), -1).astype(hb.dtype)
        act = jax.nn.gelu(jnp.einsum('td,thed->the', hb, u[eid]), approximate=False)
        return jnp.einsum('the,thed->td', g * act, v[eid])

    return lax.map(block, tokens).reshape(shape)


def setup_inputs(seed: int = 0) -> dict:
    key = jax.random.key(seed)
    ks = jax.random.split(key, 26)
    f32 = jnp.float32

    def nrm(k, shape, s):
        return jax.random.normal(k, shape, f32) * s

    d = D_MODEL
    f_bias = 3.0 + 3.0 * jnp.arange(H_M, dtype=f32) / (H_M - 1)
    gate_offset = jnp.stack([jnp.zeros((H_M,), f32), f_bias])[None, None]
    return {
        "x": nrm(ks[0], (BATCH, SEQ, d), 1.0),
        "c": nrm(ks[1], (BATCH, d), 1.0),
        "ctx": nrm(ks[2], (BATCH, CTX_LEN, d), 1.0),
        "c_ctx": nrm(ks[3], (d,), 1.0),
        "w_mod": nrm(ks[4], (DEPTH, d, 6 * d), 0.5 * d ** -0.5),
        "b_mod": nrm(ks[5], (DEPTH, 6 * d), 0.02),
        "ln_w": 1.0 + nrm(ks[6], (DEPTH, 2, d), 0.02),
        "ln_b": nrm(ks[7], (DEPTH, 2, d), 0.02),
        "ab_w_in": nrm(ks[8], (N_EVEN, d, sum(AB_WIDTHS)), d ** -0.5),
        "ab_gate_b": nrm(ks[9], (N_EVEN, 2, 2, H_M), 0.1) + gate_offset,
        "ab_norm_w": 1.0 + nrm(ks[10], (N_EVEN, M_W), 0.02),
        "ab_sink": nrm(ks[11], (N_EVEN, H_A), 0.5),
        "ab_w_out": nrm(ks[12], (N_EVEN, MIX_W, d), DEEPNORM_BETA * MIX_W ** -0.5),
        "gla_w_in": nrm(ks[13], (N_ODD, d, sum(C_WIDTHS)), d ** -0.5),
        "gla_gate_up": nrm(ks[14], (N_ODD, 2, GATE_RANK, C_K), GATE_RANK ** -0.5),
        "gla_gate_b": 1.0 + nrm(ks[15], (N_ODD, 2, C_K), 0.5),
        "gla_norm_w": 1.0 + nrm(ks[16], (N_ODD, C_V), 0.02),
        "gla_w_out": nrm(ks[17], (N_ODD, C_V, d), DEEPNORM_BETA * C_V ** -0.5),
        "peer_wq": nrm(ks[18], (DEPTH, d, PEER_HEADS * PEER_KEY_DIM), d ** -0.5),
        "peer_keys": nrm(ks[19], (DEPTH, PEER_HEADS, 2, N_KEYS, PEER_HALF), PEER_HALF ** -0.5),
        "peer_u": nrm(ks[20], (DEPTH, N_EXPERTS, d), d ** -0.5),
        "peer_v": nrm(ks[21], (DEPTH, N_EXPERTS, d), DEEPNORM_BETA * (PEER_HEADS * PEER_TOPK) ** -0.5),
    }


def reference(x, c, ctx, c_ctx, w_mod, b_mod, ln_w, ln_b, ab_w_in, ab_gate_b, ab_norm_w, ab_sink, ab_w_out,
              gla_w_in, gla_gate_up, gla_gate_b, gla_norm_w, gla_w_out, peer_wq, peer_keys, peer_u, peer_v):
    for layer in range(DEPTH):
        last = layer == DEPTH - 1
        j = layer // 2
        mod_l = jax.nn.silu(c) @ w_mod[layer] + b_mod[layer]
        mod_c = jax.nn.silu(c_ctx) @ w_mod[layer] + b_mod[layer]
        sh1, sc1, g1, sh2, sc2, g2 = jnp.split(mod_l[:, None, :], 6, axis=-1)
        csh1, csc1, cg1, csh2, csc2, cg2 = jnp.split(mod_c, 6)
        hl = x * (1.0 + sc1) + sh1
        hc = ctx * (1.0 + csc1) + csh1
        if layer % 2 == 0:
            y_lat, y_ctx = _mixer_ab(hl, hc, ab_w_in[j], ab_gate_b[j], ab_norm_w[j], ab_sink[j], ab_w_out[j], not last)
        else:
            y_lat, y_ctx = _mixer_c(hl, hc, gla_w_in[j], gla_gate_up[j], gla_gate_b[j], gla_norm_w[j], gla_w_out[j],
                                    not last)
        x = _layernorm(DEEPNORM_ALPHA * x + g1 * y_lat, ln_w[layer, 0], ln_b[layer, 0])
        f_lat = _peer(x * (1.0 + sc2) + sh2, peer_wq[layer], peer_keys[layer], peer_u[layer], peer_v[layer])
        x = _layernorm(DEEPNORM_ALPHA * x + g2 * f_lat, ln_w[layer, 1], ln_b[layer, 1])
        if not last:
            ctx = _layernorm(DEEPNORM_ALPHA * ctx + cg1 * y_ctx, ln_w[layer, 0], ln_b[layer, 0])
            f_ctx = _peer(ctx * (1.0 + csc2) + csh2, peer_wq[layer], peer_keys[layer], peer_u[layer], peer_v[layer])
            ctx = _layernorm(DEEPNORM_ALPHA * ctx + cg2 * f_ctx, ln_w[layer, 1], ln_b[layer, 1])
    return x
```

```python
import functools

import jax
import jax.numpy as jnp
import numpy as np
from jax import lax
from jax.experimental import pallas as pl
from jax.experimental.pallas import tpu as pltpu

D_MODEL = 1024
DEPTH = 2
GRID_W = 64
H_M = 4
DH_M = 128
M_W = H_M * DH_M
H_A = 8
H_KV = 2
DH_A = 64
A_Q = H_A * DH_A
A_KV = H_KV * DH_A
GQA_GROUP = H_A // H_KV
WINDOW = 128
WIN_BLOCK = 128
ROPE_THETA = 10000.0
AB_WIDTHS = (M_W, M_W, M_W, M_W, 4 * H_M, A_Q, A_KV, A_KV)
MIX_W = M_W + A_Q
H_C = 4
DK_C = 128
DV_C = 256
C_K = H_C * DK_C
C_V = H_C * DV_C
GATE_RANK = 16
GATE_TAU = 16.0
C_WIDTHS = (C_K, C_K, C_V, C_V, 2 * GATE_RANK)
CHUNK = 64
N_KEYS = 128
PEER_HEADS = 8
PEER_TOPK = 16
PEER_KEY_DIM = 256
PEER_HALF = PEER_KEY_DIM // 2
PEER_CHUNK = 128
DEEPNORM_ALPHA = (2 * DEPTH) ** 0.25
LN_EPS = 1e-5

LANE = 128
MM_TILE_M = 512
MM_TILE_N = 512


def _matmul_body(x_ref, w_ref, o_ref):
    o_ref[...] = jnp.dot(x_ref[...].astype(jnp.bfloat16), w_ref[...].astype(jnp.bfloat16),
                         preferred_element_type=jnp.float32)


def _matmul(x, w):
    m, k = x.shape
    n = w.shape[1]
    n_pad = -n % MM_TILE_N
    if n_pad:
        w = jnp.pad(w, ((0, 0), (0, n_pad)))
    tm = min(MM_TILE_M, m)
    assert m % tm == 0
    out = pl.pallas_call(
        _matmul_body,
        grid=((n + n_pad) // MM_TILE_N, m // tm),
        in_specs=[pl.BlockSpec((tm, k), lambda j, i: (i, 0)),
                  pl.BlockSpec((k, MM_TILE_N), lambda j, i: (0, j))],
        out_specs=pl.BlockSpec((tm, MM_TILE_N), lambda j, i: (i, j)),
        out_shape=jax.ShapeDtypeStruct((m, n + n_pad), jnp.float32),
        name="proj_matmul",
    )(x, w)
    return out[:, :n] if n_pad else out


def _proj(h, w):
    lead = h.shape[:-1]
    return _matmul(h.reshape(-1, h.shape[-1]), w).reshape(lead + (w.shape[1],))


def _layernorm(x, w, b):
    mu = x.mean(-1, keepdims=True)
    var = jnp.mean(jnp.square(x - mu), -1, keepdims=True)
    return ((x - mu) * lax.rsqrt(var + LN_EPS)) * w + b


def _split(a, widths):
    idx = np.cumsum(widths)[:-1].tolist()
    return jnp.split(a, idx, axis=-1)


def _heads(a, h):
    b, s = a.shape[:2]
    return a.reshape(b, s, h, -1).transpose(0, 2, 1, 3)


def _head_norm(h, w):
    hf = h * lax.rsqrt(jnp.mean(h * h, -1, keepdims=True) + LN_EPS)
    b, nh, s, d = h.shape
    return hf.transpose(0, 2, 1, 3).reshape(b, s, nh * d) * w


def _to_chunks(a):
    b, h, s = a.shape[:3]
    a = a.reshape((b, h, s // CHUNK, CHUNK) + a.shape[3:])
    return jnp.moveaxis(a, 2, 0)


def _from_chunks(a):
    a = jnp.moveaxis(a, 0, 2)
    return a.reshape(a.shape[:2] + (-1,) + a.shape[4:])


def _mlstm_scan(q, k, v, log_i, log_f, state):
    causal = jnp.tril(jnp.ones((CHUNK, CHUNK), bool))

    def step(carry, inp):
        c_st, n_st, m_st = carry
        qc, kc, vc, li, lf = inp
        cum = jnp.cumsum(lf, axis=-1)
        dmat = cum[..., :, None] - cum[..., None, :] + li[..., None, :]
        dmat = jnp.where(causal, dmat, -jnp.inf)
        m_inter = cum + m_st[..., None]
        m_t = jnp.maximum(m_inter, dmat.max(-1))
        w = jnp.exp(dmat - m_t[..., None]) * jnp.einsum('bhtd,bhsd->bhts', qc, kc)
        a_inter = jnp.exp(m_inter - m_t)
        num = a_inter[..., None] * jnp.einsum('bhtd,bhde->bhte', qc, c_st) + jnp.einsum('bhts,bhse->bhte', w, vc)
        den = a_inter * jnp.einsum('bhtd,bhd->bht', qc, n_st) + w.sum(-1)
        h = num / jnp.maximum(jnp.abs(den), jnp.exp(-m_t))[..., None]
        total = cum[..., -1]
        decay_s = total[..., None] - cum + li
        m_new = jnp.maximum(total + m_st, decay_s.max(-1))
        ws = jnp.exp(decay_s - m_new[..., None])
        a_st = jnp.exp(total + m_st - m_new)
        c_st = a_st[..., None, None] * c_st + jnp.einsum('bhs,bhsd,bhse->bhde', ws, kc, vc)
        n_st = a_st[..., None] * n_st + jnp.einsum('bhs,bhsd->bhd', ws, kc)
        return (c_st, n_st, m_new), h

    state, hs = lax.scan(step, state, tuple(_to_chunks(a) for a in (q, k, v, log_i, log_f)))
    return _from_chunks(hs), state


def _gla_scan(q, k, v, log_a, state):
    causal = jnp.tril(jnp.ones((CHUNK, CHUNK), bool))

    def step(st, inp):
        qc, kc, vc, la = inp
        cum = jnp.cumsum(la, axis=2)
        rel = cum[:, :, :, None, :] - cum[:, :, None, :, :]
        rel = jnp.where(causal[:, :, None], rel, -jnp.inf)
        scores = jnp.einsum('bhtk,bhsk,bhtsk->bhts', qc, kc, jnp.exp(rel))
        out = jnp.einsum('bhtk,bhkv->bhtv', qc * jnp.exp(cum), st) + jnp.einsum('bhts,bhsv->bhtv', scores, vc)
        total = cum[:, :, -1:, :]
        st = jnp.exp(total[:, :, 0, :, None]) * st + jnp.einsum('bhsk,bhsv->bhkv', kc * jnp.exp(total - cum), vc)
        return st, out

    state, outs = lax.scan(step, state, tuple(_to_chunks(a) for a in (q, k, v, log_a)))
    return _from_chunks(outs), state


def _directional(scan_fn, ctx_seq, lat_seq, init, reverse):
    if reverse:
        ctx_seq = tuple(jnp.flip(a, 2) for a in ctx_seq)
        lat_seq = tuple(jnp.flip(a, 2) for a in lat_seq)
    h_ctx, state = scan_fn(*ctx_seq, init)
    h_lat, _ = scan_fn(*lat_seq, state)
    if reverse:
        h_ctx, h_lat = jnp.flip(h_ctx, 2), jnp.flip(h_lat, 2)
    return h_ctx, h_lat


def _axial_rope(s):
    rows = s // GRID_W
    row = jnp.repeat(jnp.arange(rows), GRID_W).astype(jnp.float32)
    col = jnp.tile(jnp.arange(GRID_W), rows).astype(jnp.float32)
    n_freq = DH_A // 4
    inv = ROPE_THETA ** (-jnp.arange(n_freq, dtype=jnp.float32) / n_freq)
    ang = jnp.concatenate([row[:, None] * inv, col[:, None] * inv], -1)
    return jnp.cos(ang), jnp.sin(ang)


def _rope(x, cos, sin):
    x1, x2 = x[..., 0::2], x[..., 1::2]
    c = cos[None, :, None, :]
    sn = sin[None, :, None, :]
    return jnp.stack([x1 * c - x2 * sn, x1 * sn + x2 * c], -1).reshape(x.shape)


def _window_attention(q, k, v, k_ctx, v_ctx, sink):
    b, s = q.shape[:2]
    nb = s // WIN_BLOCK
    lc = k_ctx.shape[1]
    qb = jnp.moveaxis(q.reshape(b, nb, WIN_BLOCK, H_KV, GQA_GROUP, DH_A), 1, 0)

    def bands(a):
        ap = jnp.pad(a, ((0, 0), (WIN_BLOCK, WIN_BLOCK), (0, 0), (0, 0))).reshape(b, nb + 2, WIN_BLOCK, H_KV, DH_A)
        return jnp.moveaxis(jnp.concatenate([ap[:, :-2], ap[:, 1:-1], ap[:, 2:]], 2), 1, 0)

    kw, vw = bands(k), bands(v)
    qi = jnp.arange(WIN_BLOCK)[:, None]
    kj = jnp.arange(3 * WIN_BLOCK)[None, :]
    key_pos = jnp.arange(nb)[:, None, None] * WIN_BLOCK + kj - WIN_BLOCK
    valid = (jnp.abs(kj - WIN_BLOCK - qi) <= WINDOW) & (key_pos >= 0) & (key_pos < s)
    scale = DH_A ** -0.5
    sink_l = sink.reshape(H_KV, GQA_GROUP)

    def block(args):
        qn, kn, vn, mask = args
        s_ctx = jnp.einsum('bqhgd,bchd->bhgqc', qn, k_ctx) * scale
        s_win = jnp.einsum('bqhgd,bkhd->bhgqk', qn, kn) * scale
        s_win = jnp.where(mask, s_win, -jnp.inf)
        sink_col = jnp.broadcast_to(sink_l[None, :, :, None, None], s_ctx.shape[:-1] + (1,))
        p = jax.nn.softmax(jnp.concatenate([sink_col, s_ctx, s_win], -1), -1)
        return (jnp.einsum('bhgqc,bchd->bqhgd', p[..., 1:1 + lc], v_ctx)
                + jnp.einsum('bhgqk,bkhd->bqhgd', p[..., 1 + lc:], vn))

    out = lax.map(block, (qb, kw, vw, valid))
    return jnp.moveaxis(out, 0, 1).reshape(b, s, H_A, DH_A)


def _context_attention(q, k, v, sink):
    b, lc = q.shape[:2]
    qg = q.reshape(b, lc, H_KV, GQA_GROUP, DH_A)
    sc = jnp.einsum('bqhgd,bkhd->bhgqk', qg, k) * DH_A ** -0.5
    sink_col = jnp.broadcast_to(sink.reshape(H_KV, GQA_GROUP)[None, :, :, None, None], sc.shape[:-1] + (1,))
    p = jax.nn.softmax(jnp.concatenate([sink_col, sc], -1), -1)[..., 1:]
    return jnp.einsum('bhgqk,bkhd->bqhgd', p, v).reshape(b, lc, H_A, DH_A)


def _ab_streams(h, w_in, gate_b):
    q_m, k_m, v_m, o_m, g_m, q_a, k_a, v_a = _split(_proj(h, w_in), AB_WIDTHS)
    b, s = h.shape[:2]
    g = g_m.reshape(b, s, 2, 2, H_M) + gate_b
    mlstm = (_heads(q_m, H_M), _heads(k_m, H_M) * DH_M ** -0.5, _heads(v_m, H_M))
    gates = [(g[:, :, d, 0].transpose(0, 2, 1), jax.nn.log_sigmoid(g[:, :, d, 1]).transpose(0, 2, 1))
             for d in range(2)]
    attn = (q_a.reshape(b, s, H_A, DH_A), k_a.reshape(b, s, H_KV, DH_A), v_a.reshape(b, s, H_KV, DH_A))
    return mlstm, gates, o_m, attn


def _merge_ab(m, o, a, norm_w, w_out):
    hm = _head_norm(m, norm_w) * jax.nn.sigmoid(o)
    cat = jnp.concatenate([hm, a.reshape(a.shape[0], a.shape[1], A_Q)], -1)
    return _proj(cat, w_out)


def _mixer_ab(hl, hc, w_in, gate_b, norm_w, sink, w_out, ctx_out):
    ml, gl, ol, (ql, kl, vl) = _ab_streams(hl, w_in, gate_b)
    mc, gc, oc, (qc, kc, vc) = _ab_streams(hc, w_in, gate_b)
    b = hl.shape[0]
    init = (jnp.zeros((b, H_M, DH_M, DH_M), jnp.float32), jnp.zeros((b, H_M, DH_M), jnp.float32),
            jnp.zeros((b, H_M), jnp.float32))
    outs = [_directional(_mlstm_scan, mc + gc[d], ml + gl[d], init, d == 1) for d in range(2)]
    cos, sin = _axial_rope(hl.shape[1])
    a_lat = _window_attention(_rope(ql, cos, sin), _rope(kl, cos, sin), vl, kc, vc, sink)
    y_lat = _merge_ab(outs[0][1] + outs[1][1], ol, a_lat, norm_w, w_out)
    if not ctx_out:
        return y_lat, None
    y_ctx = _merge_ab(outs[0][0] + outs[1][0], oc, _context_attention(qc, kc, vc, sink), norm_w, w_out)
    return y_lat, y_ctx


def _c_streams(h, w_in, gate_up, gate_b):
    q, k, v, g, low = _split(_proj(h, w_in), C_WIDTHS)
    b, s = h.shape[:2]
    qkv = (_heads(q, H_C) * DK_C ** -0.5, _heads(k, H_C), _heads(v, H_C))
    low = low.reshape(b, s, 2, GATE_RANK)
    log_a = [_heads(jax.nn.log_sigmoid(low[:, :, d] @ gate_up[d] + gate_b[d]) / GATE_TAU, H_C) for d in range(2)]
    return qkv, log_a, g


def _merge_c(h, g, norm_w, w_out):
    return _proj(_head_norm(h, norm_w) * jax.nn.silu(g), w_out)


def _mixer_c(hl, hc, w_in, gate_up, gate_b, norm_w, w_out, ctx_out):
    sl, al, gl = _c_streams(hl, w_in, gate_up, gate_b)
    sc, ac, gc = _c_streams(hc, w_in, gate_up, gate_b)
    init = jnp.zeros((hl.shape[0], H_C, DK_C, DV_C), jnp.float32)
    outs = [_directional(_gla_scan, sc + (ac[d],), sl + (al[d],), init, d == 1) for d in range(2)]
    y_lat = _merge_c(outs[0][1] + outs[1][1], gl, norm_w, w_out)
    if not ctx_out:
        return y_lat, None
    return y_lat, _merge_c(outs[0][0] + outs[1][0], gc, norm_w, w_out)


def _peer(h, wq, keys, u, v):
    shape = h.shape
    q_all = _proj(h, wq).reshape(-1, PEER_CHUNK, PEER_HEADS, 2, PEER_HALF)
    tokens = h.reshape(-1, PEER_CHUNK, shape[-1])

    def block(args):
        hb, q = args
        tc = hb.shape[0]
        s = jnp.einsum('thpk,hpnk->thpn', q, keys)
        s_top, i_top = lax.top_k(s, PEER_TOPK)
        cand = s_top[:, :, 0, :, None] + s_top[:, :, 1, None, :]
        cid = i_top[:, :, 0, :, None] * N_KEYS + i_top[:, :, 1, None, :]
        best, pos = lax.top_k(cand.reshape(tc, PEER_HEADS, -1), PEER_TOPK)
        eid = jnp.take_along_axis(cid.reshape(tc, PEER_HEADS, -1), pos, -1)
        g = jax.nn.softmax(best, -1)
        act = jax.nn.gelu(jnp.einsum('td,thed->the', hb, u[eid]), approximate=False)
        return jnp.einsum('the,thed->td', g * act, v[eid])

    return lax.map(block, (tokens, q_all)).reshape(shape)


def kernel(x, c, ctx, c_ctx, w_mod, b_mod, ln_w, ln_b, ab_w_in, ab_gate_b, ab_norm_w, ab_sink, ab_w_out,
           gla_w_in, gla_gate_up, gla_gate_b, gla_norm_w, gla_w_out, peer_wq, peer_keys, peer_u, peer_v):
    for layer in range(DEPTH):
        last = layer == DEPTH - 1
        j = layer // 2
        mod_l = jax.nn.silu(c) @ w_mod[layer] + b_mod[layer]
        mod_c = jax.nn.silu(c_ctx) @ w_mod[layer] + b_mod[layer]
        sh1, sc1, g1, sh2, sc2, g2 = jnp.split(mod_l[:, None, :], 6, axis=-1)
        csh1, csc1, cg1, csh2, csc2, cg2 = jnp.split(mod_c, 6)
        hl = x * (1.0 + sc1) + sh1
        hc = ctx * (1.0 + csc1) + csh1
        if layer % 2 == 0:
            y_lat, y_ctx = _mixer_ab(hl, hc, ab_w_in[j], ab_gate_b[j], ab_norm_w[j], ab_sink[j], ab_w_out[j], not last)
        else:
            y_lat, y_ctx = _mixer_c(hl, hc, gla_w_in[j], gla_gate_up[j], gla_gate_b[j], gla_norm_w[j], gla_w_out[j],
                                    not last)
        x = _layernorm(DEEPNORM_ALPHA * x + g1 * y_lat, ln_w[layer, 0], ln_b[layer, 0])
        f_lat = _peer(x * (1.0 + sc2) + sh2, peer_wq[layer], peer_keys[layer], peer_u[layer], peer_v[layer])
        x = _layernorm(DEEPNORM_ALPHA * x + g2 * f_lat, ln_w[layer, 1], ln_b[layer, 1])
        if not last:
            ctx = _layernorm(DEEPNORM_ALPHA * ctx + cg1 * y_ctx, ln_w[layer, 0], ln_b[layer, 0])
            f_ctx = _peer(ctx * (1.0 + csc2) + csh2, peer_wq[layer], peer_keys[layer], peer_u[layer], peer_v[layer])
            ctx = _layernorm(DEEPNORM_ALPHA * ctx + cg2 * f_ctx, ln_w[layer, 1], ln_b[layer, 1])
    return x
```

```python
import functools

import jax
import jax.numpy as jnp
import numpy as np
from jax import lax
from jax.experimental import pallas as pl
from jax.experimental.pallas import tpu as pltpu

D_MODEL = 1024
DEPTH = 2
GRID_W = 64
H_M = 4
DH_M = 128
M_W = H_M * DH_M
H_A = 8
H_KV = 2
DH_A = 64
A_Q = H_A * DH_A
A_KV = H_KV * DH_A
GQA_GROUP = H_A // H_KV
WINDOW = 128
WIN_BLOCK = 128
ROPE_THETA = 10000.0
AB_WIDTHS = (M_W, M_W, M_W, M_W, 4 * H_M, A_Q, A_KV, A_KV)
MIX_W = M_W + A_Q
H_C = 4
DK_C = 128
DV_C = 256
C_K = H_C * DK_C
C_V = H_C * DV_C
GATE_RANK = 16
GATE_TAU = 16.0
C_WIDTHS = (C_K, C_K, C_V, C_V, 2 * GATE_RANK)
CHUNK = 64
N_KEYS = 128
PEER_HEADS = 8
PEER_TOPK = 16
PEER_KEY_DIM = 256
PEER_HALF = PEER_KEY_DIM // 2
PEER_CHUNK = 128
DEEPNORM_ALPHA = (2 * DEPTH) ** 0.25
LN_EPS = 1e-5

LANE = 128
MM_TILE_M = 512
MM_TILE_N = 512


def _matmul_body(x_ref, w_ref, o_ref):
    o_ref[...] = jnp.dot(x_ref[...].astype(jnp.bfloat16), w_ref[...].astype(jnp.bfloat16),
                         preferred_element_type=jnp.float32)


def _matmul(x, w):
    m, k = x.shape
    n = w.shape[1]
    n_pad = -n % MM_TILE_N
    if n_pad:
        w = jnp.pad(w, ((0, 0), (0, n_pad)))
    tm = min(MM_TILE_M, m)
    assert m % tm == 0
    out = pl.pallas_call(
        _matmul_body,
        grid=((n + n_pad) // MM_TILE_N, m // tm),
        in_specs=[pl.BlockSpec((tm, k), lambda j, i: (i, 0)),
                  pl.BlockSpec((k, MM_TILE_N), lambda j, i: (0, j))],
        out_specs=pl.BlockSpec((tm, MM_TILE_N), lambda j, i: (i, j)),
        out_shape=jax.ShapeDtypeStruct((m, n + n_pad), jnp.float32),
        name="proj_matmul",
    )(x, w)
    return out[:, :n] if n_pad else out


def _proj(h, w):
    lead = h.shape[:-1]
    return _matmul(h.reshape(-1, h.shape[-1]), w).reshape(lead + (w.shape[1],))


def _layernorm(x, w, b):
    mu = x.mean(-1, keepdims=True)
    var = jnp.mean(jnp.square(x - mu), -1, keepdims=True)
    return ((x - mu) * lax.rsqrt(var + LN_EPS)) * w + b


def _split(a, widths):
    idx = np.cumsum(widths)[:-1].tolist()
    return jnp.split(a, idx, axis=-1)


def _heads(a, h):
    b, s = a.shape[:2]
    return a.reshape(b, s, h, -1).transpose(0, 2, 1, 3)


def _head_norm(h, w):
    hf = h * lax.rsqrt(jnp.mean(h * h, -1, keepdims=True) + LN_EPS)
    b, nh, s, d = h.shape
    return hf.transpose(0, 2, 1, 3).reshape(b, s, nh * d) * w


def _to_chunks(a):
    b, h, s = a.shape[:3]
    a = a.reshape((b, h, s // CHUNK, CHUNK) + a.shape[3:])
    return jnp.moveaxis(a, 2, 0)


def _from_chunks(a):
    a = jnp.moveaxis(a, 0, 2)
    return a.reshape(a.shape[:2] + (-1,) + a.shape[4:])


def _mlstm_scan(q, k, v, log_i, log_f, state):
    causal = jnp.tril(jnp.ones((CHUNK, CHUNK), bool))

    def step(carry, inp):
        c_st, n_st, m_st = carry
        qc, kc, vc, li, lf = inp
        cum = jnp.cumsum(lf, axis=-1)
        dmat = cum[..., :, None] - cum[..., None, :] + li[..., None, :]
        dmat = jnp.where(causal, dmat, -jnp.inf)
        m_inter = cum + m_st[..., None]
        m_t = jnp.maximum(m_inter, dmat.max(-1))
        w = jnp.exp(dmat - m_t[..., None]) * jnp.einsum('bhtd,bhsd->bhts', qc, kc)
        a_inter = jnp.exp(m_inter - m_t)
        num = a_inter[..., None] * jnp.einsum('bhtd,bhde->bhte', qc, c_st) + jnp.einsum('bhts,bhse->bhte', w, vc)
        den = a_inter * jnp.einsum('bhtd,bhd->bht', qc, n_st) + w.sum(-1)
        h = num / jnp.maximum(jnp.abs(den), jnp.exp(-m_t))[..., None]
        total = cum[..., -1]
        decay_s = total[..., None] - cum + li
        m_new = jnp.maximum(total + m_st, decay_s.max(-1))
        ws = jnp.exp(decay_s - m_new[..., None])
        a_st = jnp.exp(total + m_st - m_new)
        c_st = a_st[..., None, None] * c_st + jnp.einsum('bhs,bhsd,bhse->bhde', ws, kc, vc)
        n_st = a_st[..., None] * n_st + jnp.einsum('bhs,bhsd->bhd', ws, kc)
        return (c_st, n_st, m_new), h

    state, hs = lax.scan(step, state, tuple(_to_chunks(a) for a in (q, k, v, log_i, log_f)))
    return _from_chunks(hs), state


def _gla_scan(q, k, v, log_a, state):
    causal = jnp.tril(jnp.ones((CHUNK, CHUNK), bool))

    def step(st, inp):
        qc, kc, vc, la = inp
        cum = jnp.cumsum(la, axis=2)
        rel = cum[:, :, :, None, :] - cum[:, :, None, :, :]
        rel = jnp.where(causal[:, :, None], rel, -jnp.inf)
        scores = jnp.einsum('bhtk,bhsk,bhtsk->bhts', qc, kc, jnp.exp(rel))
        out = jnp.einsum('bhtk,bhkv->bhtv', qc * jnp.exp(cum), st) + jnp.einsum('bhts,bhsv->bhtv', scores, vc)
        total = cum[:, :, -1:, :]
        st = jnp.exp(total[:, :, 0, :, None]) * st + jnp.einsum('bhsk,bhsv->bhkv', kc * jnp.exp(total - cum), vc)
        return st, out

    state, outs = lax.scan(step, state, tuple(_to_chunks(a) for a in (q, k, v, log_a)))
    return _from_chunks(outs), state


def _directional(scan_fn, ctx_seq, lat_seq, init, reverse):
    if reverse:
        ctx_seq = tuple(jnp.flip(a, 2) for a in ctx_seq)
        lat_seq = tuple(jnp.flip(a, 2) for a in lat_seq)
    h_ctx, state = scan_fn(*ctx_seq, init)
    h_lat, _ = scan_fn(*lat_seq, state)
    if reverse:
        h_ctx, h_lat = jnp.flip(h_ctx, 2), jnp.flip(h_lat, 2)
    return h_ctx, h_lat


def _axial_rope(s):
    rows = s // GRID_W
    row = jnp.repeat(jnp.arange(rows), GRID_W).astype(jnp.float32)
    col = jnp.tile(jnp.arange(GRID_W), rows).astype(jnp.float32)
    n_freq = DH_A // 4
    inv = ROPE_THETA ** (-jnp.arange(n_freq, dtype=jnp.float32) / n_freq)
    ang = jnp.concatenate([row[:, None] * inv, col[:, None] * inv], -1)
    return jnp.cos(ang), jnp.sin(ang)


def _rope(x, cos, sin):
    x1, x2 = x[..., 0::2], x[..., 1::2]
    c = cos[None, :, None, :]
    sn = sin[None, :, None, :]
    return jnp.stack([x1 * c - x2 * sn, x1 * sn + x2 * c], -1).reshape(x.shape)


def _window_attention(q, k, v, k_ctx, v_ctx, sink):
    b, s = q.shape[:2]
    nb = s // WIN_BLOCK
    lc = k_ctx.shape[1]
    qb = jnp.moveaxis(q.reshape(b, nb, WIN_BLOCK, H_KV, GQA_GROUP, DH_A), 1, 0)

    def bands(a):
        ap = jnp.pad(a, ((0, 0), (WIN_BLOCK, WIN_BLOCK), (0, 0), (0, 0))).reshape(b, nb + 2, WIN_BLOCK, H_KV, DH_A)
        return jnp.moveaxis(jnp.concatenate([ap[:, :-2], ap[:, 1:-1], ap[:, 2:]], 2), 1, 0)

    kw, vw = bands(k), bands(v)
    qi = jnp.arange(WIN_BLOCK)[:, None]
    kj = jnp.arange(3 * WIN_BLOCK)[None, :]
    key_pos = jnp.arange(nb)[:, None, None] * WIN_BLOCK + kj - WIN_BLOCK
    valid = (jnp.abs(kj - WIN_BLOCK - qi) <= WINDOW) & (key_pos >= 0) & (key_pos < s)
    scale = DH_A ** -0.5
    sink_l = sink.reshape(H_KV, GQA_GROUP)

    def block(args):
        qn, kn, vn, mask = args
        s_ctx = jnp.einsum('bqhgd,bchd->bhgqc', qn, k_ctx) * scale
        s_win = jnp.einsum('bqhgd,bkhd->bhgqk', qn, kn) * scale
        s_win = jnp.where(mask, s_win, -jnp.inf)
        sink_col = jnp.broadcast_to(sink_l[None, :, :, None, None], s_ctx.shape[:-1] + (1,))
        p = jax.nn.softmax(jnp.concatenate([sink_col, s_ctx, s_win], -1), -1)
        return (jnp.einsum('bhgqc,bchd->bqhgd', p[..., 1:1 + lc], v_ctx)
                + jnp.einsum('bhgqk,bkhd->bqhgd', p[..., 1 + lc:], vn))

    out = lax.map(block, (qb, kw, vw, valid))
    return jnp.moveaxis(out, 0, 1).reshape(b, s, H_A, DH_A)


def _context_attention(q, k, v, sink):
    b, lc = q.shape[:2]
    qg = q.reshape(b, lc, H_KV, GQA_GROUP, DH_A)
    sc = jnp.einsum('bqhgd,bkhd->bhgqk', qg, k) * DH_A ** -0.5
    sink_col = jnp.broadcast_to(sink.reshape(H_KV, GQA_GROUP)[None, :, :, None, None], sc.shape[:-1] + (1,))
    p = jax.nn.softmax(jnp.concatenate([sink_col, sc], -1), -1)[..., 1:]
    return jnp.einsum('bhgqk,bkhd->bqhgd', p, v).reshape(b, lc, H_A, DH_A)


def _ab_streams(h, w_in, gate_b):
    q_m, k_m, v_m, o_m, g_m, q_a, k_a, v_a = _split(_proj(h, w_in), AB_WIDTHS)
    b, s = h.shape[:2]
    g = g_m.reshape(b, s, 2, 2, H_M) + gate_b
    mlstm = (_heads(q_m, H_M), _heads(k_m, H_M) * DH_M ** -0.5, _heads(v_m, H_M))
    gates = [(g[:, :, d, 0].transpose(0, 2, 1), jax.nn.log_sigmoid(g[:, :, d, 1]).transpose(0, 2, 1))
             for d in range(2)]
    attn = (q_a.reshape(b, s, H_A, DH_A), k_a.reshape(b, s, H_KV, DH_A), v_a.reshape(b, s, H_KV, DH_A))
    return mlstm, gates, o_m, attn


def _merge_ab(m, o, a, norm_w, w_out):
    hm = _head_norm(m, norm_w) * jax.nn.sigmoid(o)
    cat = jnp.concatenate([hm, a.reshape(a.shape[0], a.shape[1], A_Q)], -1)
    return _proj(cat, w_out)


def _mixer_ab(hl, hc, w_in, gate_b, norm_w, sink, w_out, ctx_out):
    ml, gl, ol, (ql, kl, vl) = _ab_streams(hl, w_in, gate_b)
    mc, gc, oc, (qc, kc, vc) = _ab_streams(hc, w_in, gate_b)
    b = hl.shape[0]
    init = (jnp.zeros((b, H_M, DH_M, DH_M), jnp.float32), jnp.zeros((b, H_M, DH_M), jnp.float32),
            jnp.zeros((b, H_M), jnp.float32))
    outs = [_directional(_mlstm_scan, mc + gc[d], ml + gl[d], init, d == 1) for d in range(2)]
    cos, sin = _axial_rope(hl.shape[1])
    a_lat = _window_attention(_rope(ql, cos, sin), _rope(kl, cos, sin), vl, kc, vc, sink)
    y_lat = _merge_ab(outs[0][1] + outs[1][1], ol, a_lat, norm_w, w_out)
    if not ctx_out:
        return y_lat, None
    y_ctx = _merge_ab(outs[0][0] + outs[1][0], oc, _context_attention(qc, kc, vc, sink), norm_w, w_out)
    return y_lat, y_ctx


def _c_streams(h, w_in, gate_up, gate_b):
    q, k, v, g, low = _split(_proj(h, w_in), C_WIDTHS)
    b, s = h.shape[:2]
    qkv = (_heads(q, H_C) * DK_C ** -0.5, _heads(k, H_C), _heads(v, H_C))
    low = low.reshape(b, s, 2, GATE_RANK)
    log_a = [_heads(jax.nn.log_sigmoid(low[:, :, d] @ gate_up[d] + gate_b[d]) / GATE_TAU, H_C) for d in range(2)]
    return qkv, log_a, g


def _merge_c(h, g, norm_w, w_out):
    return _proj(_head_norm(h, norm_w) * jax.nn.silu(g), w_out)


def _mixer_c(hl, hc, w_in, gate_up, gate_b, norm_w, w_out, ctx_out):
    sl, al, gl = _c_streams(hl, w_in, gate_up, gate_b)
    sc, ac, gc = _c_streams(hc, w_in, gate_up, gate_b)
    init = jnp.zeros((hl.shape[0], H_C, DK_C, DV_C), jnp.float32)
    outs = [_directional(_gla_scan, sc + (ac[d],), sl + (al[d],), init, d == 1) for d in range(2)]
    y_lat = _merge_c(outs[0][1] + outs[1][1], gl, norm_w, w_out)
    if not ctx_out:
        return y_lat, None
    return y_lat, _merge_c(outs[0][0] + outs[1][0], gc, norm_w, w_out)


PEER_SLOTS = 8
PEER_AHEAD = 6
PEER_TB = 256
N_PICK = PEER_HEADS * PEER_TOPK
MXU_ROWS = 16
SQRT_HALF = 0.7071067811865476
UV_ROWS = 2 * D_MODEL // LANE


def _hi_lo_rows(row):
    n = row.shape[1]
    hi = row.astype(jnp.bfloat16).astype(jnp.float32)
    lo = row - hi
    sub = lax.broadcasted_iota(jnp.int32, (MXU_ROWS, n), 0)
    full = jnp.where(sub == 0, jnp.broadcast_to(hi, (MXU_ROWS, n)),
                     jnp.where(sub == 1, jnp.broadcast_to(lo, (MXU_ROWS, n)), 0.0))
    return full.astype(jnp.bfloat16)


def _peer_expert_body(eid_hbm, h_ref, gate_ref, uv_hbm, out_ref, eid_s, buf, row_sems, eid_sem):
    i = pl.program_id(0)
    nsteps = pl.num_programs(0)
    cur = i % 2
    nxt = 1 - cur

    def eid_copy(block, slot):
        return pltpu.make_async_copy(eid_hbm.at[block], eid_s.at[slot], eid_sem)

    def issue(eslot, t, slot):
        for e in range(N_PICK):
            pltpu.make_async_copy(uv_hbm.at[eid_s[eslot, t, e]],
                                  buf.at[slot, pl.ds(e * UV_ROWS, UV_ROWS), :], row_sems.at[slot]).start()

    def wait_rows(slot):
        pltpu.make_async_copy(buf.at[slot], buf.at[slot], row_sems.at[slot]).wait()

    @pl.when(i == 0)
    def _():
        first = eid_copy(0, 0)
        first.start()
        first.wait()
        for d in range(PEER_AHEAD):
            issue(0, d, d)

    nxt_block = jnp.minimum(i + 1, nsteps - 1)
    eid_copy(nxt_block, nxt).start()

    def token(t, slot, eslot_ahead, t_ahead):
        issue(eslot_ahead, t_ahead, (slot + PEER_AHEAD) % PEER_SLOTS)
        wait_rows(slot)
        hl = _hi_lo_rows(h_ref[pl.ds(t, 1), :])
        z2 = jnp.zeros((MXU_ROWS, N_PICK), jnp.float32)
        for s in range(D_MODEL // LANE):
            gu = buf[slot, pl.ds(s, N_PICK, stride=UV_ROWS), :].astype(jnp.bfloat16)
            z2 = z2 + lax.dot_general(hl[:, s * LANE:(s + 1) * LANE], gu, (((1,), (1,)), ((), ())),
                                      preferred_element_type=jnp.float32)
        z = z2[0:1] + z2[1:2]
        act = 0.5 * z * (1.0 + lax.erf(z * SQRT_HALF))
        al = _hi_lo_rows(gate_ref[pl.ds(t, 1), :] * act)
        chunks = []
        for s in range(D_MODEL // LANE):
            gv = buf[slot, pl.ds(D_MODEL // LANE + s, N_PICK, stride=UV_ROWS), :].astype(jnp.bfloat16)
            o2 = jnp.dot(al, gv, preferred_element_type=jnp.float32)
            chunks.append(o2[0:1] + o2[1:2])
        out_ref[pl.ds(t, 1), :] = jnp.concatenate(chunks, axis=1)

    def group(base, eslot_ahead, base_ahead):
        for k in range(PEER_SLOTS):
            token(base + k, k, eslot_ahead, base_ahead + k)

    def body_cur(gi, carry):
        base = gi * PEER_SLOTS
        group(base, cur, base + PEER_AHEAD)
        return carry

    n_groups = PEER_TB // PEER_SLOTS
    lax.fori_loop(0, n_groups - 1, body_cur, 0)
    eid_copy(nxt_block, nxt).wait()
    last = (n_groups - 1) * PEER_SLOTS
    for k in range(PEER_SLOTS):
        ahead = last + k + PEER_AHEAD
        if ahead < PEER_TB:
            token(last + k, k, cur, ahead)
        else:
            token(last + k, k, nxt, ahead - PEER_TB)

    @pl.when(i == nsteps - 1)
    def _():
        for d in range(PEER_AHEAD):
            wait_rows(d)


def _peer_experts(h, eid, gate, uv):
    t = h.shape[0]
    assert t % PEER_TB == 0 and PEER_TB % PEER_SLOTS == 0 and PEER_AHEAD < PEER_SLOTS
    nb = t // PEER_TB
    return pl.pallas_call(
        _peer_expert_body,
        grid=(nb,),
        in_specs=[pl.BlockSpec(memory_space=pl.ANY),
                  pl.BlockSpec((PEER_TB, D_MODEL), lambda i: (i, 0)),
                  pl.BlockSpec((PEER_TB, N_PICK), lambda i: (i, 0)),
                  pl.BlockSpec(memory_space=pl.ANY)],
        out_specs=pl.BlockSpec((PEER_TB, D_MODEL), lambda i: (i, 0)),
        out_shape=jax.ShapeDtypeStruct((t, D_MODEL), jnp.float32),
        scratch_shapes=[pltpu.SMEM((2, PEER_TB, N_PICK), jnp.int32),
                        pltpu.VMEM((PEER_SLOTS, N_PICK * UV_ROWS, LANE), jnp.float32),
                        pltpu.SemaphoreType.DMA((PEER_SLOTS,)),
                        pltpu.SemaphoreType.DMA],
        compiler_params=pltpu.CompilerParams(dimension_semantics=("arbitrary",)),
        name="peer_experts",
    )(eid.reshape(nb, PEER_TB, N_PICK), h, gate, uv)


def _peer_select(h, wq, keys):
    t = h.shape[0]
    q = _matmul(h, wq).reshape(t, PEER_HEADS, 2, PEER_HALF)
    s = jnp.einsum('thpk,hpnk->thpn', q, keys)
    s_top, i_top = lax.top_k(s, PEER_TOPK)
    cand = s_top[:, :, 0, :, None] + s_top[:, :, 1, None, :]
    cid = i_top[:, :, 0, :, None] * N_KEYS + i_top[:, :, 1, None, :]
    best, pos = lax.top_k(cand.reshape(t, PEER_HEADS, -1), PEER_TOPK)
    eid = jnp.take_along_axis(cid.reshape(t, PEER_HEADS, -1), pos, -1)
    g = jax.nn.softmax(best, -1)
    return eid.reshape(t, N_PICK).astype(jnp.int32), g.reshape(t, N_PICK)


def _peer(h, wq, keys, uv):
    shape = h.shape
    h2 = h.reshape(-1, shape[-1])
    eid, gate = _peer_select(h2, wq, keys)
    return _peer_experts(h2, eid, gate, uv).reshape(shape)


def kernel(x, c, ctx, c_ctx, w_mod, b_mod, ln_w, ln_b, ab_w_in, ab_gate_b, ab_norm_w, ab_sink, ab_w_out,
           gla_w_in, gla_gate_up, gla_gate_b, gla_norm_w, gla_w_out, peer_wq, peer_keys, peer_u, peer_v):
    for layer in range(DEPTH):
        last = layer == DEPTH - 1
        j = layer // 2
        mod_l = jax.nn.silu(c) @ w_mod[layer] + b_mod[layer]
        mod_c = jax.nn.silu(c_ctx) @ w_mod[layer] + b_mod[layer]
        sh1, sc1, g1, sh2, sc2, g2 = jnp.split(mod_l[:, None, :], 6, axis=-1)
        csh1, csc1, cg1, csh2, csc2, cg2 = jnp.split(mod_c, 6)
        hl = x * (1.0 + sc1) + sh1
        hc = ctx * (1.0 + csc1) + csh1
        if layer % 2 == 0:
            y_lat, y_ctx = _mixer_ab(hl, hc, ab_w_in[j], ab_gate_b[j], ab_norm_w[j], ab_sink[j], ab_w_out[j], not last)
        else:
            y_lat, y_ctx = _mixer_c(hl, hc, gla_w_in[j], gla_gate_up[j], gla_gate_b[j], gla_norm_w[j], gla_w_out[j],
                                    not last)
        x = _layernorm(DEEPNORM_ALPHA * x + g1 * y_lat, ln_w[layer, 0], ln_b[layer, 0])
        uv = jnp.concatenate([peer_u[layer].reshape(-1, UV_ROWS // 2, LANE),
                              peer_v[layer].reshape(-1, UV_ROWS // 2, LANE)], axis=1)
        f_lat = _peer(x * (1.0 + sc2) + sh2, peer_wq[layer], peer_keys[layer], uv)
        x = _layernorm(DEEPNORM_ALPHA * x + g2 * f_lat, ln_w[layer, 1], ln_b[layer, 1])
        if not last:
            ctx = _layernorm(DEEPNORM_ALPHA * ctx + cg1 * y_ctx, ln_w[layer, 0], ln_b[layer, 0])
            f_ctx = _peer(ctx * (1.0 + csc2) + csh2, peer_wq[layer], peer_keys[layer], uv)
            ctx = _layernorm(DEEPNORM_ALPHA * ctx + cg2 * f_ctx, ln_w[layer, 1], ln_b[layer, 1])
    return x
```

```python
import functools

import jax
import jax.numpy as jnp
import numpy as np
from jax import lax
from jax.experimental import pallas as pl
from jax.experimental.pallas import tpu as pltpu

D_MODEL = 1024
DEPTH = 2
GRID_W = 64
H_M = 4
DH_M = 128
M_W = H_M * DH_M
H_A = 8
H_KV = 2
DH_A = 64
A_Q = H_A * DH_A
A_KV = H_KV * DH_A
GQA_GROUP = H_A // H_KV
WINDOW = 128
WIN_BLOCK = 128
ROPE_THETA = 10000.0
AB_WIDTHS = (M_W, M_W, M_W, M_W, 4 * H_M, A_Q, A_KV, A_KV)
MIX_W = M_W + A_Q
H_C = 4
DK_C = 128
DV_C = 256
C_K = H_C * DK_C
C_V = H_C * DV_C
GATE_RANK = 16
GATE_TAU = 16.0
C_WIDTHS = (C_K, C_K, C_V, C_V, 2 * GATE_RANK)
CHUNK = 64
N_KEYS = 128
PEER_HEADS = 8
PEER_TOPK = 16
PEER_KEY_DIM = 256
PEER_HALF = PEER_KEY_DIM // 2
PEER_CHUNK = 128
DEEPNORM_ALPHA = (2 * DEPTH) ** 0.25
LN_EPS = 1e-5

LANE = 128
MM_TILE_M = 512
MM_TILE_N = 512


def _matmul_body(x_ref, w_ref, o_ref):
    o_ref[...] = jnp.dot(x_ref[...].astype(jnp.bfloat16), w_ref[...].astype(jnp.bfloat16),
                         preferred_element_type=jnp.float32)


def _matmul(x, w):
    m, k = x.shape
    n = w.shape[1]
    n_pad = -n % MM_TILE_N
    if n_pad:
        w = jnp.pad(w, ((0, 0), (0, n_pad)))
    tm = min(MM_TILE_M, m)
    assert m % tm == 0
    out = pl.pallas_call(
        _matmul_body,
        grid=((n + n_pad) // MM_TILE_N, m // tm),
        in_specs=[pl.BlockSpec((tm, k), lambda j, i: (i, 0)),
                  pl.BlockSpec((k, MM_TILE_N), lambda j, i: (0, j))],
        out_specs=pl.BlockSpec((tm, MM_TILE_N), lambda j, i: (i, j)),
        out_shape=jax.ShapeDtypeStruct((m, n + n_pad), jnp.float32),
        name="proj_matmul",
    )(x, w)
    return out[:, :n] if n_pad else out


def _proj(h, w):
    lead = h.shape[:-1]
    return _matmul(h.reshape(-1, h.shape[-1]), w).reshape(lead + (w.shape[1],))


def _layernorm(x, w, b):
    mu = x.mean(-1, keepdims=True)
    var = jnp.mean(jnp.square(x - mu), -1, keepdims=True)
    return ((x - mu) * lax.rsqrt(var + LN_EPS)) * w + b


def _split(a, widths):
    idx = np.cumsum(widths)[:-1].tolist()
    return jnp.split(a, idx, axis=-1)


def _heads(a, h):
    b, s = a.shape[:2]
    return a.reshape(b, s, h, -1).transpose(0, 2, 1, 3)


def _head_norm(h, w):
    hf = h * lax.rsqrt(jnp.mean(h * h, -1, keepdims=True) + LN_EPS)
    b, nh, s, d = h.shape
    return hf.transpose(0, 2, 1, 3).reshape(b, s, nh * d) * w


def _to_chunks(a):
    b, h, s = a.shape[:3]
    a = a.reshape((b, h, s // CHUNK, CHUNK) + a.shape[3:])
    return jnp.moveaxis(a, 2, 0)


def _from_chunks(a):
    a = jnp.moveaxis(a, 0, 2)
    return a.reshape(a.shape[:2] + (-1,) + a.shape[4:])


def _mlstm_scan(q, k, v, log_i, log_f, state):
    causal = jnp.tril(jnp.ones((CHUNK, CHUNK), bool))

    def step(carry, inp):
        c_st, n_st, m_st = carry
        qc, kc, vc, li, lf = inp
        cum = jnp.cumsum(lf, axis=-1)
        dmat = cum[..., :, None] - cum[..., None, :] + li[..., None, :]
        dmat = jnp.where(causal, dmat, -jnp.inf)
        m_inter = cum + m_st[..., None]
        m_t = jnp.maximum(m_inter, dmat.max(-1))
        w = jnp.exp(dmat - m_t[..., None]) * jnp.einsum('bhtd,bhsd->bhts', qc, kc)
        a_inter = jnp.exp(m_inter - m_t)
        num = a_inter[..., None] * jnp.einsum('bhtd,bhde->bhte', qc, c_st) + jnp.einsum('bhts,bhse->bhte', w, vc)
        den = a_inter * jnp.einsum('bhtd,bhd->bht', qc, n_st) + w.sum(-1)
        h = num / jnp.maximum(jnp.abs(den), jnp.exp(-m_t))[..., None]
        total = cum[..., -1]
        decay_s = total[..., None] - cum + li
        m_new = jnp.maximum(total + m_st, decay_s.max(-1))
        ws = jnp.exp(decay_s - m_new[..., None])
        a_st = jnp.exp(total + m_st - m_new)
        c_st = a_st[..., None, None] * c_st + jnp.einsum('bhs,bhsd,bhse->bhde', ws, kc, vc)
        n_st = a_st[..., None] * n_st + jnp.einsum('bhs,bhsd->bhd', ws, kc)
        return (c_st, n_st, m_new), h

    state, hs = lax.scan(step, state, tuple(_to_chunks(a) for a in (q, k, v, log_i, log_f)))
    return _from_chunks(hs), state


def _gla_scan(q, k, v, log_a, state):
    causal = jnp.tril(jnp.ones((CHUNK, CHUNK), bool))

    def step(st, inp):
        qc, kc, vc, la = inp
        cum = jnp.cumsum(la, axis=2)
        rel = cum[:, :, :, None, :] - cum[:, :, None, :, :]
        rel = jnp.where(causal[:, :, None], rel, -jnp.inf)
        scores = jnp.einsum('bhtk,bhsk,bhtsk->bhts', qc, kc, jnp.exp(rel))
        out = jnp.einsum('bhtk,bhkv->bhtv', qc * jnp.exp(cum), st) + jnp.einsum('bhts,bhsv->bhtv', scores, vc)
        total = cum[:, :, -1:, :]
        st = jnp.exp(total[:, :, 0, :, None]) * st + jnp.einsum('bhsk,bhsv->bhkv', kc * jnp.exp(total - cum), vc)
        return st, out

    state, outs = lax.scan(step, state, tuple(_to_chunks(a) for a in (q, k, v, log_a)))
    return _from_chunks(outs), state


def _directional(scan_fn, ctx_seq, lat_seq, init, reverse):
    if reverse:
        ctx_seq = tuple(jnp.flip(a, 2) for a in ctx_seq)
        lat_seq = tuple(jnp.flip(a, 2) for a in lat_seq)
    h_ctx, state = scan_fn(*ctx_seq, init)
    h_lat, _ = scan_fn(*lat_seq, state)
    if reverse:
        h_ctx, h_lat = jnp.flip(h_ctx, 2), jnp.flip(h_lat, 2)
    return h_ctx, h_lat


def _axial_rope(s):
    rows = s // GRID_W
    row = jnp.repeat(jnp.arange(rows), GRID_W).astype(jnp.float32)
    col = jnp.tile(jnp.arange(GRID_W), rows).astype(jnp.float32)
    n_freq = DH_A // 4
    inv = ROPE_THETA ** (-jnp.arange(n_freq, dtype=jnp.float32) / n_freq)
    ang = jnp.concatenate([row[:, None] * inv, col[:, None] * inv], -1)
    return jnp.cos(ang), jnp.sin(ang)


def _rope(x, cos, sin):
    x1, x2 = x[..., 0::2], x[..., 1::2]
    c = cos[None, :, None, :]
    sn = sin[None, :, None, :]
    return jnp.stack([x1 * c - x2 * sn, x1 * sn + x2 * c], -1).reshape(x.shape)


def _window_attention(q, k, v, k_ctx, v_ctx, sink):
    b, s = q.shape[:2]
    nb = s // WIN_BLOCK
    lc = k_ctx.shape[1]
    qb = jnp.moveaxis(q.reshape(b, nb, WIN_BLOCK, H_KV, GQA_GROUP, DH_A), 1, 0)

    def bands(a):
        ap = jnp.pad(a, ((0, 0), (WIN_BLOCK, WIN_BLOCK), (0, 0), (0, 0))).reshape(b, nb + 2, WIN_BLOCK, H_KV, DH_A)
        return jnp.moveaxis(jnp.concatenate([ap[:, :-2], ap[:, 1:-1], ap[:, 2:]], 2), 1, 0)

    kw, vw = bands(k), bands(v)
    qi = jnp.arange(WIN_BLOCK)[:, None]
    kj = jnp.arange(3 * WIN_BLOCK)[None, :]
    key_pos = jnp.arange(nb)[:, None, None] * WIN_BLOCK + kj - WIN_BLOCK
    valid = (jnp.abs(kj - WIN_BLOCK - qi) <= WINDOW) & (key_pos >= 0) & (key_pos < s)
    scale = DH_A ** -0.5
    sink_l = sink.reshape(H_KV, GQA_GROUP)

    def block(args):
        qn, kn, vn, mask = args
        s_ctx = jnp.einsum('bqhgd,bchd->bhgqc', qn, k_ctx) * scale
        s_win = jnp.einsum('bqhgd,bkhd->bhgqk', qn, kn) * scale
        s_win = jnp.where(mask, s_win, -jnp.inf)
        sink_col = jnp.broadcast_to(sink_l[None, :, :, None, None], s_ctx.shape[:-1] + (1,))
        p = jax.nn.softmax(jnp.concatenate([sink_col, s_ctx, s_win], -1), -1)
        return (jnp.einsum('bhgqc,bchd->bqhgd', p[..., 1:1 + lc], v_ctx)
                + jnp.einsum('bhgqk,bkhd->bqhgd', p[..., 1 + lc:], vn))

    out = lax.map(block, (qb, kw, vw, valid))
    return jnp.moveaxis(out, 0, 1).reshape(b, s, H_A, DH_A)


def _context_attention(q, k, v, sink):
    b, lc = q.shape[:2]
    qg = q.reshape(b, lc, H_KV, GQA_GROUP, DH_A)
    sc = jnp.einsum('bqhgd,bkhd->bhgqk', qg, k) * DH_A ** -0.5
    sink_col = jnp.broadcast_to(sink.reshape(H_KV, GQA_GROUP)[None, :, :, None, None], sc.shape[:-1] + (1,))
    p = jax.nn.softmax(jnp.concatenate([sink_col, sc], -1), -1)[..., 1:]
    return jnp.einsum('bhgqk,bkhd->bqhgd', p, v).reshape(b, lc, H_A, DH_A)


def _ab_streams(h, w_in, gate_b):
    q_m, k_m, v_m, o_m, g_m, q_a, k_a, v_a = _split(_proj(h, w_in), AB_WIDTHS)
    b, s = h.shape[:2]
    g = g_m.reshape(b, s, 2, 2, H_M) + gate_b
    mlstm = (_heads(q_m, H_M), _heads(k_m, H_M) * DH_M ** -0.5, _heads(v_m, H_M))
    gates = [(g[:, :, d, 0].transpose(0, 2, 1), jax.nn.log_sigmoid(g[:, :, d, 1]).transpose(0, 2, 1))
             for d in range(2)]
    attn = (q_a.reshape(b, s, H_A, DH_A), k_a.reshape(b, s, H_KV, DH_A), v_a.reshape(b, s, H_KV, DH_A))
    return mlstm, gates, o_m, attn


def _merge_ab(m, o, a, norm_w, w_out):
    hm = _head_norm(m, norm_w) * jax.nn.sigmoid(o)
    cat = jnp.concatenate([hm, a.reshape(a.shape[0], a.shape[1], A_Q)], -1)
    return _proj(cat, w_out)


def _mixer_ab(hl, hc, w_in, gate_b, norm_w, sink, w_out, ctx_out):
    ml, gl, ol, (ql, kl, vl) = _ab_streams(hl, w_in, gate_b)
    mc, gc, oc, (qc, kc, vc) = _ab_streams(hc, w_in, gate_b)
    b = hl.shape[0]
    init = (jnp.zeros((b, H_M, DH_M, DH_M), jnp.float32), jnp.zeros((b, H_M, DH_M), jnp.float32),
            jnp.zeros((b, H_M), jnp.float32))
    outs = [_directional(_mlstm_scan, mc + gc[d], ml + gl[d], init, d == 1) for d in range(2)]
    cos, sin = _axial_rope(hl.shape[1])
    a_lat = _window_attention(_rope(ql, cos, sin), _rope(kl, cos, sin), vl, kc, vc, sink)
    y_lat = _merge_ab(outs[0][1] + outs[1][1], ol, a_lat, norm_w, w_out)
    if not ctx_out:
        return y_lat, None
    y_ctx = _merge_ab(outs[0][0] + outs[1][0], oc, _context_attention(qc, kc, vc, sink), norm_w, w_out)
    return y_lat, y_ctx


def _c_streams(h, w_in, gate_up, gate_b):
    q, k, v, g, low = _split(_proj(h, w_in), C_WIDTHS)
    b, s = h.shape[:2]
    qkv = (_heads(q, H_C) * DK_C ** -0.5, _heads(k, H_C), _heads(v, H_C))
    low = low.reshape(b, s, 2, GATE_RANK)
    log_a = [_heads(jax.nn.log_sigmoid(low[:, :, d] @ gate_up[d] + gate_b[d]) / GATE_TAU, H_C) for d in range(2)]
    return qkv, log_a, g


def _merge_c(h, g, norm_w, w_out):
    return _proj(_head_norm(h, norm_w) * jax.nn.silu(g), w_out)


def _mixer_c(hl, hc, w_in, gate_up, gate_b, norm_w, w_out, ctx_out):
    sl, al, gl = _c_streams(hl, w_in, gate_up, gate_b)
    sc, ac, gc = _c_streams(hc, w_in, gate_up, gate_b)
    init = jnp.zeros((hl.shape[0], H_C, DK_C, DV_C), jnp.float32)
    outs = [_directional(_gla_scan, sc + (ac[d],), sl + (al[d],), init, d == 1) for d in range(2)]
    y_lat = _merge_c(outs[0][1] + outs[1][1], gl, norm_w, w_out)
    if not ctx_out:
        return y_lat, None
    return y_lat, _merge_c(outs[0][0] + outs[1][0], gc, norm_w, w_out)


PEER_GROUP = 8
PEER_RING = 4
PEER_AHEAD = 3
PEER_TB = 256
PEER_VMEM_BYTES = (PEER_RING * PEER_GROUP * 2 * D_MODEL * PEER_HEADS * PEER_TOPK * 4
                   + 2 * PEER_TB * (2 * D_MODEL + PEER_HEADS * PEER_TOPK) * 4 + (8 << 20))
N_PICK = PEER_HEADS * PEER_TOPK
MXU_ROWS = 16
SQRT_HALF = 0.7071067811865476
UV_ROWS = 2 * D_MODEL // LANE


def _hi_lo_rows(row):
    n = row.shape[1]
    hi = row.astype(jnp.bfloat16).astype(jnp.float32)
    lo = row - hi
    sub = lax.broadcasted_iota(jnp.int32, (MXU_ROWS, n), 0)
    full = jnp.where(sub == 0, jnp.broadcast_to(hi, (MXU_ROWS, n)),
                     jnp.where(sub == 1, jnp.broadcast_to(lo, (MXU_ROWS, n)), 0.0))
    return full.astype(jnp.bfloat16)


def _peer_expert_body(eid_hbm, h_ref, gate_ref, uv_hbm, out_ref, eid_s, *rest):
    bufs, (row_sems, eid_sem) = rest[:PEER_RING], rest[PEER_RING:]
    i = pl.program_id(0)
    nsteps = pl.num_programs(0)
    cur = i % 2
    nxt = 1 - cur
    n_groups = PEER_TB // PEER_GROUP
    ahead_tokens = PEER_AHEAD * PEER_GROUP

    def eid_copies(step, slot):
        blk = jnp.minimum(step, nsteps - 1)
        tail = jnp.minimum(step + 1, nsteps - 1)
        return (pltpu.make_async_copy(eid_hbm.at[blk], eid_s.at[slot, pl.ds(0, PEER_TB)], eid_sem),
                pltpu.make_async_copy(eid_hbm.at[tail, pl.ds(0, ahead_tokens)],
                                      eid_s.at[slot, pl.ds(PEER_TB, ahead_tokens)], eid_sem))

    def issue_token(base, k, which):
        for e in range(N_PICK):
            pltpu.make_async_copy(uv_hbm.at[eid_s[cur, base + k, e]], bufs[which].at[k, :, e, :],
                                  row_sems.at[which]).start()

    def issue(base, which):
        for k in range(PEER_GROUP):
            issue_token(base, k, which)

    def wait_rows(which):
        pltpu.make_async_copy(bufs[which], bufs[which], row_sems.at[which]).wait()

    @pl.when(i == 0)
    def _():
        for c in eid_copies(0, 0):
            c.start()
        for c in eid_copies(0, 0):
            c.wait()
        for g in range(PEER_AHEAD):
            issue(g * PEER_GROUP, g)

    for c in eid_copies(i + 1, nxt):
        c.start()

    n_chunk = D_MODEL // LANE

    def compute(base, which):
        buf = bufs[which]
        base = pl.multiple_of(base, PEER_GROUP)
        h8 = h_ref[pl.ds(base, PEER_GROUP), :]
        g8 = gate_ref[pl.ds(base, PEER_GROUP), :]
        rows = []
        for k in range(PEER_GROUP):
            issue_token(base + ahead_tokens, k, (which + PEER_AHEAD) % PEER_RING)
            gu = jnp.concatenate([buf[k, s] for s in range(n_chunk)], axis=1).astype(jnp.bfloat16)
            z2 = lax.dot_general(_hi_lo_rows(h8[k:k + 1]), gu, (((1,), (1,)), ((), ())),
                                 preferred_element_type=jnp.float32)
            z = z2[0:1] + z2[1:2]
            act = 0.5 * z * (1.0 + lax.erf(z * SQRT_HALF))
            gv = jnp.concatenate([buf[k, n_chunk + s] for s in range(n_chunk)], axis=1).astype(jnp.bfloat16)
            o2 = jnp.dot(_hi_lo_rows(g8[k:k + 1] * act), gv, preferred_element_type=jnp.float32)
            rows.append(o2[0:1] + o2[1:2])
        out_ref[pl.ds(base, PEER_GROUP), :] = jnp.concatenate(rows, axis=0)

    def ring_turn(j, carry):
        for w in range(PEER_RING):
            wait_rows(w)
            compute((j * PEER_RING + w) * PEER_GROUP, w)
        return carry

    lax.fori_loop(0, n_groups // PEER_RING, ring_turn, 0)
    for c in eid_copies(i + 1, nxt):
        c.wait()

    @pl.when(i == nsteps - 1)
    def _():
        for g in range(PEER_AHEAD):
            wait_rows(g)


def _peer_experts(h, eid, gate, uv):
    t = h.shape[0]
    assert t % PEER_TB == 0 and PEER_TB % (PEER_RING * PEER_GROUP) == 0 and 0 < PEER_AHEAD < PEER_RING
    nb = t // PEER_TB
    group_buf = pltpu.VMEM((PEER_GROUP, UV_ROWS, N_PICK, LANE), jnp.float32)
    return pl.pallas_call(
        _peer_expert_body,
        grid=(nb,),
        in_specs=[pl.BlockSpec(memory_space=pl.ANY),
                  pl.BlockSpec((PEER_TB, D_MODEL), lambda i: (i, 0)),
                  pl.BlockSpec((PEER_TB, N_PICK), lambda i: (i, 0)),
                  pl.BlockSpec(memory_space=pl.ANY)],
        out_specs=pl.BlockSpec((PEER_TB, D_MODEL), lambda i: (i, 0)),
        out_shape=jax.ShapeDtypeStruct((t, D_MODEL), jnp.float32),
        scratch_shapes=[pltpu.SMEM((2, PEER_TB + PEER_AHEAD * PEER_GROUP, N_PICK), jnp.int32)]
        + [group_buf] * PEER_RING
        + [pltpu.SemaphoreType.DMA((PEER_RING,)), pltpu.SemaphoreType.DMA],
        compiler_params=pltpu.CompilerParams(dimension_semantics=("arbitrary",),
                                             vmem_limit_bytes=PEER_VMEM_BYTES),
        name="peer_experts",
    )(eid.reshape(nb, PEER_TB, N_PICK), h, gate, uv)


def _peer_select(h, wq, keys):
    t = h.shape[0]
    q = _matmul(h, wq).reshape(t, PEER_HEADS, 2, PEER_HALF)
    s = jnp.einsum('thpk,hpnk->thpn', q, keys)
    s_top, i_top = lax.top_k(s, PEER_TOPK)
    cand = s_top[:, :, 0, :, None] + s_top[:, :, 1, None, :]
    cid = i_top[:, :, 0, :, None] * N_KEYS + i_top[:, :, 1, None, :]
    best, pos = lax.top_k(cand.reshape(t, PEER_HEADS, -1), PEER_TOPK)
    eid = jnp.take_along_axis(cid.reshape(t, PEER_HEADS, -1), pos, -1)
    g = jax.nn.softmax(best, -1)
    return eid.reshape(t, N_PICK).astype(jnp.int32), g.reshape(t, N_PICK)


def _peer(h, wq, keys, uv):
    shape = h.shape
    h2 = h.reshape(-1, shape[-1])
    eid, gate = _peer_select(h2, wq, keys)
    return _peer_experts(h2, eid, gate, uv).reshape(shape)


def kernel(x, c, ctx, c_ctx, w_mod, b_mod, ln_w, ln_b, ab_w_in, ab_gate_b, ab_norm_w, ab_sink, ab_w_out,
           gla_w_in, gla_gate_up, gla_gate_b, gla_norm_w, gla_w_out, peer_wq, peer_keys, peer_u, peer_v):
    for layer in range(DEPTH):
        last = layer == DEPTH - 1
        j = layer // 2
        mod_l = jax.nn.silu(c) @ w_mod[layer] + b_mod[layer]
        mod_c = jax.nn.silu(c_ctx) @ w_mod[layer] + b_mod[layer]
        sh1, sc1, g1, sh2, sc2, g2 = jnp.split(mod_l[:, None, :], 6, axis=-1)
        csh1, csc1, cg1, csh2, csc2, cg2 = jnp.split(mod_c, 6)
        hl = x * (1.0 + sc1) + sh1
        hc = ctx * (1.0 + csc1) + csh1
        if layer % 2 == 0:
            y_lat, y_ctx = _mixer_ab(hl, hc, ab_w_in[j], ab_gate_b[j], ab_norm_w[j], ab_sink[j], ab_w_out[j], not last)
        else:
            y_lat, y_ctx = _mixer_c(hl, hc, gla_w_in[j], gla_gate_up[j], gla_gate_b[j], gla_norm_w[j], gla_w_out[j],
                                    not last)
        x = _layernorm(DEEPNORM_ALPHA * x + g1 * y_lat, ln_w[layer, 0], ln_b[layer, 0])
        uv = jnp.concatenate([peer_u[layer].reshape(-1, UV_ROWS // 2, LANE),
                              peer_v[layer].reshape(-1, UV_ROWS // 2, LANE)], axis=1)
        f_lat = _peer(x * (1.0 + sc2) + sh2, peer_wq[layer], peer_keys[layer], uv)
        x = _layernorm(DEEPNORM_ALPHA * x + g2 * f_lat, ln_w[layer, 1], ln_b[layer, 1])
        if not last:
            ctx = _layernorm(DEEPNORM_ALPHA * ctx + cg1 * y_ctx, ln_w[layer, 0], ln_b[layer, 0])
            f_ctx = _peer(ctx * (1.0 + csc2) + csh2, peer_wq[layer], peer_keys[layer], uv)
            ctx = _layernorm(DEEPNORM_ALPHA * ctx + cg2 * f_ctx, ln_w[layer, 1], ln_b[layer, 1])
    return x
```

```python
import functools

import jax
import jax.numpy as jnp
import numpy as np
from jax import lax
from jax.experimental import pallas as pl
from jax.experimental.pallas import tpu as pltpu

D_MODEL = 1024
DEPTH = 2
GRID_W = 64
H_M = 4
DH_M = 128
M_W = H_M * DH_M
H_A = 8
H_KV = 2
DH_A = 64
A_Q = H_A * DH_A
A_KV = H_KV * DH_A
GQA_GROUP = H_A // H_KV
WINDOW = 128
WIN_BLOCK = 128
ROPE_THETA = 10000.0
AB_WIDTHS = (M_W, M_W, M_W, M_W, 4 * H_M, A_Q, A_KV, A_KV)
MIX_W = M_W + A_Q
H_C = 4
DK_C = 128
DV_C = 256
C_K = H_C * DK_C
C_V = H_C * DV_C
GATE_RANK = 16
GATE_TAU = 16.0
C_WIDTHS = (C_K, C_K, C_V, C_V, 2 * GATE_RANK)
CHUNK = 64
N_KEYS = 128
PEER_HEADS = 8
PEER_TOPK = 16
PEER_KEY_DIM = 256
PEER_HALF = PEER_KEY_DIM // 2
PEER_CHUNK = 128
DEEPNORM_ALPHA = (2 * DEPTH) ** 0.25
LN_EPS = 1e-5

LANE = 128
MM_TILE_M = 512
MM_TILE_N = 512


def _matmul_body(x_ref, w_ref, o_ref):
    o_ref[...] = jnp.dot(x_ref[...].astype(jnp.bfloat16), w_ref[...].astype(jnp.bfloat16),
                         preferred_element_type=jnp.float32)


def _matmul(x, w):
    m, k = x.shape
    n = w.shape[1]
    n_pad = -n % MM_TILE_N
    if n_pad:
        w = jnp.pad(w, ((0, 0), (0, n_pad)))
    tm = min(MM_TILE_M, m)
    assert m % tm == 0
    out = pl.pallas_call(
        _matmul_body,
        grid=((n + n_pad) // MM_TILE_N, m // tm),
        in_specs=[pl.BlockSpec((tm, k), lambda j, i: (i, 0)),
                  pl.BlockSpec((k, MM_TILE_N), lambda j, i: (0, j))],
        out_specs=pl.BlockSpec((tm, MM_TILE_N), lambda j, i: (i, j)),
        out_shape=jax.ShapeDtypeStruct((m, n + n_pad), jnp.float32),
        name="proj_matmul",
    )(x, w)
    return out[:, :n] if n_pad else out


def _proj(h, w):
    lead = h.shape[:-1]
    return _matmul(h.reshape(-1, h.shape[-1]), w).reshape(lead + (w.shape[1],))


def _layernorm(x, w, b):
    mu = x.mean(-1, keepdims=True)
    var = jnp.mean(jnp.square(x - mu), -1, keepdims=True)
    return ((x - mu) * lax.rsqrt(var + LN_EPS)) * w + b


def _split(a, widths):
    idx = np.cumsum(widths)[:-1].tolist()
    return jnp.split(a, idx, axis=-1)


def _heads(a, h):
    b, s = a.shape[:2]
    return a.reshape(b, s, h, -1).transpose(0, 2, 1, 3)


def _head_norm(h, w):
    hf = h * lax.rsqrt(jnp.mean(h * h, -1, keepdims=True) + LN_EPS)
    b, nh, s, d = h.shape
    return hf.transpose(0, 2, 1, 3).reshape(b, s, nh * d) * w


def _to_chunks(a):
    b, h, s = a.shape[:3]
    a = a.reshape((b, h, s // CHUNK, CHUNK) + a.shape[3:])
    return jnp.moveaxis(a, 2, 0)


def _from_chunks(a):
    a = jnp.moveaxis(a, 0, 2)
    return a.reshape(a.shape[:2] + (-1,) + a.shape[4:])


def _mlstm_scan(q, k, v, log_i, log_f, state):
    causal = jnp.tril(jnp.ones((CHUNK, CHUNK), bool))

    def step(carry, inp):
        c_st, n_st, m_st = carry
        qc, kc, vc, li, lf = inp
        cum = jnp.cumsum(lf, axis=-1)
        dmat = cum[..., :, None] - cum[..., None, :] + li[..., None, :]
        dmat = jnp.where(causal, dmat, -jnp.inf)
        m_inter = cum + m_st[..., None]
        m_t = jnp.maximum(m_inter, dmat.max(-1))
        w = jnp.exp(dmat - m_t[..., None]) * jnp.einsum('bhtd,bhsd->bhts', qc, kc)
        a_inter = jnp.exp(m_inter - m_t)
        num = a_inter[..., None] * jnp.einsum('bhtd,bhde->bhte', qc, c_st) + jnp.einsum('bhts,bhse->bhte', w, vc)
        den = a_inter * jnp.einsum('bhtd,bhd->bht', qc, n_st) + w.sum(-1)
        h = num / jnp.maximum(jnp.abs(den), jnp.exp(-m_t))[..., None]
        total = cum[..., -1]
        decay_s = total[..., None] - cum + li
        m_new = jnp.maximum(total + m_st, decay_s.max(-1))
        ws = jnp.exp(decay_s - m_new[..., None])
        a_st = jnp.exp(total + m_st - m_new)
        c_st = a_st[..., None, None] * c_st + jnp.einsum('bhs,bhsd,bhse->bhde', ws, kc, vc)
        n_st = a_st[..., None] * n_st + jnp.einsum('bhs,bhsd->bhd', ws, kc)
        return (c_st, n_st, m_new), h

    state, hs = lax.scan(step, state, tuple(_to_chunks(a) for a in (q, k, v, log_i, log_f)))
    return _from_chunks(hs), state


def _gla_scan(q, k, v, log_a, state):
    causal = jnp.tril(jnp.ones((CHUNK, CHUNK), bool))

    def step(st, inp):
        qc, kc, vc, la = inp
        cum = jnp.cumsum(la, axis=2)
        rel = cum[:, :, :, None, :] - cum[:, :, None, :, :]
        rel = jnp.where(causal[:, :, None], rel, -jnp.inf)
        scores = jnp.einsum('bhtk,bhsk,bhtsk->bhts', qc, kc, jnp.exp(rel))
        out = jnp.einsum('bhtk,bhkv->bhtv', qc * jnp.exp(cum), st) + jnp.einsum('bhts,bhsv->bhtv', scores, vc)
        total = cum[:, :, -1:, :]
        st = jnp.exp(total[:, :, 0, :, None]) * st + jnp.einsum('bhsk,bhsv->bhkv', kc * jnp.exp(total - cum), vc)
        return st, out

    state, outs = lax.scan(step, state, tuple(_to_chunks(a) for a in (q, k, v, log_a)))
    return _from_chunks(outs), state


def _directional(scan_fn, ctx_seq, lat_seq, init, reverse):
    if reverse:
        ctx_seq = tuple(jnp.flip(a, 2) for a in ctx_seq)
        lat_seq = tuple(jnp.flip(a, 2) for a in lat_seq)
    h_ctx, state = scan_fn(*ctx_seq, init)
    h_lat, _ = scan_fn(*lat_seq, state)
    if reverse:
        h_ctx, h_lat = jnp.flip(h_ctx, 2), jnp.flip(h_lat, 2)
    return h_ctx, h_lat


def _axial_rope(s):
    rows = s // GRID_W
    row = jnp.repeat(jnp.arange(rows), GRID_W).astype(jnp.float32)
    col = jnp.tile(jnp.arange(GRID_W), rows).astype(jnp.float32)
    n_freq = DH_A // 4
    inv = ROPE_THETA ** (-jnp.arange(n_freq, dtype=jnp.float32) / n_freq)
    ang = jnp.concatenate([row[:, None] * inv, col[:, None] * inv], -1)
    return jnp.cos(ang), jnp.sin(ang)


def _rope(x, cos, sin):
    x1, x2 = x[..., 0::2], x[..., 1::2]
    c = cos[None, :, None, :]
    sn = sin[None, :, None, :]
    return jnp.stack([x1 * c - x2 * sn, x1 * sn + x2 * c], -1).reshape(x.shape)


def _window_attention(q, k, v, k_ctx, v_ctx, sink):
    b, s = q.shape[:2]
    nb = s // WIN_BLOCK
    lc = k_ctx.shape[1]
    qb = jnp.moveaxis(q.reshape(b, nb, WIN_BLOCK, H_KV, GQA_GROUP, DH_A), 1, 0)

    def bands(a):
        ap = jnp.pad(a, ((0, 0), (WIN_BLOCK, WIN_BLOCK), (0, 0), (0, 0))).reshape(b, nb + 2, WIN_BLOCK, H_KV, DH_A)
        return jnp.moveaxis(jnp.concatenate([ap[:, :-2], ap[:, 1:-1], ap[:, 2:]], 2), 1, 0)

    kw, vw = bands(k), bands(v)
    qi = jnp.arange(WIN_BLOCK)[:, None]
    kj = jnp.arange(3 * WIN_BLOCK)[None, :]
    key_pos = jnp.arange(nb)[:, None, None] * WIN_BLOCK + kj - WIN_BLOCK
    valid = (jnp.abs(kj - WIN_BLOCK - qi) <= WINDOW) & (key_pos >= 0) & (key_pos < s)
    scale = DH_A ** -0.5
    sink_l = sink.reshape(H_KV, GQA_GROUP)

    def block(args):
        qn, kn, vn, mask = args
        s_ctx = jnp.einsum('bqhgd,bchd->bhgqc', qn, k_ctx) * scale
        s_win = jnp.einsum('bqhgd,bkhd->bhgqk', qn, kn) * scale
        s_win = jnp.where(mask, s_win, -jnp.inf)
        sink_col = jnp.broadcast_to(sink_l[None, :, :, None, None], s_ctx.shape[:-1] + (1,))
        p = jax.nn.softmax(jnp.concatenate([sink_col, s_ctx, s_win], -1), -1)
        return (jnp.einsum('bhgqc,bchd->bqhgd', p[..., 1:1 + lc], v_ctx)
                + jnp.einsum('bhgqk,bkhd->bqhgd', p[..., 1 + lc:], vn))

    out = lax.map(block, (qb, kw, vw, valid))
    return jnp.moveaxis(out, 0, 1).reshape(b, s, H_A, DH_A)


def _context_attention(q, k, v, sink):
    b, lc = q.shape[:2]
    qg = q.reshape(b, lc, H_KV, GQA_GROUP, DH_A)
    sc = jnp.einsum('bqhgd,bkhd->bhgqk', qg, k) * DH_A ** -0.5
    sink_col = jnp.broadcast_to(sink.reshape(H_KV, GQA_GROUP)[None, :, :, None, None], sc.shape[:-1] + (1,))
    p = jax.nn.softmax(jnp.concatenate([sink_col, sc], -1), -1)[..., 1:]
    return jnp.einsum('bhgqk,bkhd->bqhgd', p, v).reshape(b, lc, H_A, DH_A)


def _ab_streams(h, w_in, gate_b):
    q_m, k_m, v_m, o_m, g_m, q_a, k_a, v_a = _split(_proj(h, w_in), AB_WIDTHS)
    b, s = h.shape[:2]
    g = g_m.reshape(b, s, 2, 2, H_M) + gate_b
    mlstm = (_heads(q_m, H_M), _heads(k_m, H_M) * DH_M ** -0.5, _heads(v_m, H_M))
    gates = [(g[:, :, d, 0].transpose(0, 2, 1), jax.nn.log_sigmoid(g[:, :, d, 1]).transpose(0, 2, 1))
             for d in range(2)]
    attn = (q_a.reshape(b, s, H_A, DH_A), k_a.reshape(b, s, H_KV, DH_A), v_a.reshape(b, s, H_KV, DH_A))
    return mlstm, gates, o_m, attn


def _merge_ab(m, o, a, norm_w, w_out):
    hm = _head_norm(m, norm_w) * jax.nn.sigmoid(o)
    cat = jnp.concatenate([hm, a.reshape(a.shape[0], a.shape[1], A_Q)], -1)
    return _proj(cat, w_out)


def _mixer_ab(hl, hc, w_in, gate_b, norm_w, sink, w_out, ctx_out):
    ml, gl, ol, (ql, kl, vl) = _ab_streams(hl, w_in, gate_b)
    mc, gc, oc, (qc, kc, vc) = _ab_streams(hc, w_in, gate_b)
    b = hl.shape[0]
    init = (jnp.zeros((b, H_M, DH_M, DH_M), jnp.float32), jnp.zeros((b, H_M, DH_M), jnp.float32),
            jnp.zeros((b, H_M), jnp.float32))
    outs = [_directional(_mlstm_scan, mc + gc[d], ml + gl[d], init, d == 1) for d in range(2)]
    cos, sin = _axial_rope(hl.shape[1])
    a_lat = _window_attention(_rope(ql, cos, sin), _rope(kl, cos, sin), vl, kc, vc, sink)
    y_lat = _merge_ab(outs[0][1] + outs[1][1], ol, a_lat, norm_w, w_out)
    if not ctx_out:
        return y_lat, None
    y_ctx = _merge_ab(outs[0][0] + outs[1][0], oc, _context_attention(qc, kc, vc, sink), norm_w, w_out)
    return y_lat, y_ctx


def _c_streams(h, w_in, gate_up, gate_b):
    q, k, v, g, low = _split(_proj(h, w_in), C_WIDTHS)
    b, s = h.shape[:2]
    qkv = (_heads(q, H_C) * DK_C ** -0.5, _heads(k, H_C), _heads(v, H_C))
    low = low.reshape(b, s, 2, GATE_RANK)
    log_a = [_heads(jax.nn.log_sigmoid(low[:, :, d] @ gate_up[d] + gate_b[d]) / GATE_TAU, H_C) for d in range(2)]
    return qkv, log_a, g


def _merge_c(h, g, norm_w, w_out):
    return _proj(_head_norm(h, norm_w) * jax.nn.silu(g), w_out)


def _mixer_c(hl, hc, w_in, gate_up, gate_b, norm_w, w_out, ctx_out):
    sl, al, gl = _c_streams(hl, w_in, gate_up, gate_b)
    sc, ac, gc = _c_streams(hc, w_in, gate_up, gate_b)
    init = jnp.zeros((hl.shape[0], H_C, DK_C, DV_C), jnp.float32)
    outs = [_directional(_gla_scan, sc + (ac[d],), sl + (al[d],), init, d == 1) for d in range(2)]
    y_lat = _merge_c(outs[0][1] + outs[1][1], gl, norm_w, w_out)
    if not ctx_out:
        return y_lat, None
    return y_lat, _merge_c(outs[0][0] + outs[1][0], gc, norm_w, w_out)


PEER_GROUP = 8
PEER_RING = 4
PEER_AHEAD = 3
PEER_TB = 256
DMA_QUEUES = 2
PEER_VMEM_BYTES = (PEER_RING * PEER_GROUP * 2 * D_MODEL * PEER_HEADS * PEER_TOPK * 4
                   + 2 * PEER_TB * (2 * D_MODEL + PEER_HEADS * PEER_TOPK) * 4 + (8 << 20))
N_PICK = PEER_HEADS * PEER_TOPK
MXU_ROWS = 16
SQRT_HALF = 0.7071067811865476
UV_ROWS = 2 * D_MODEL // LANE


def _hi_lo_rows(row):
    n = row.shape[1]
    hi = row.astype(jnp.bfloat16).astype(jnp.float32)
    lo = row - hi
    sub = lax.broadcasted_iota(jnp.int32, (MXU_ROWS, n), 0)
    full = jnp.where(sub == 0, jnp.broadcast_to(hi, (MXU_ROWS, n)),
                     jnp.where(sub == 1, jnp.broadcast_to(lo, (MXU_ROWS, n)), 0.0))
    return full.astype(jnp.bfloat16)


def _peer_expert_body(eid_hbm, h_ref, gate_ref, uv_hbm, out_ref, eid_s, *rest):
    bufs, (row_sems, eid_sem) = rest[:PEER_RING], rest[PEER_RING:]
    i = pl.program_id(0)
    nsteps = pl.num_programs(0)
    cur = i % 2
    nxt = 1 - cur
    n_groups = PEER_TB // PEER_GROUP
    ahead_tokens = PEER_AHEAD * PEER_GROUP

    def eid_copies(step, slot):
        blk = jnp.minimum(step, nsteps - 1)
        tail = jnp.minimum(step + 1, nsteps - 1)
        return (pltpu.make_async_copy(eid_hbm.at[blk], eid_s.at[slot, pl.ds(0, PEER_TB)], eid_sem),
                pltpu.make_async_copy(eid_hbm.at[tail, pl.ds(0, ahead_tokens)],
                                      eid_s.at[slot, pl.ds(PEER_TB, ahead_tokens)], eid_sem))

    def issue_token(base, k, which):
        for e in range(N_PICK):
            pltpu.make_async_copy(uv_hbm.at[eid_s[cur, base + k, e]], bufs[which].at[k, :, e, :],
                                  row_sems.at[which]).start(priority=e % DMA_QUEUES)

    def issue(base, which):
        for k in range(PEER_GROUP):
            issue_token(base, k, which)

    def wait_rows(which):
        pltpu.make_async_copy(bufs[which], bufs[which], row_sems.at[which]).wait()

    @pl.when(i == 0)
    def _():
        for c in eid_copies(0, 0):
            c.start()
        for c in eid_copies(0, 0):
            c.wait()
        for g in range(PEER_AHEAD):
            issue(g * PEER_GROUP, g)

    for c in eid_copies(i + 1, nxt):
        c.start()

    n_chunk = D_MODEL // LANE

    def compute(base, which):
        buf = bufs[which]
        base = pl.multiple_of(base, PEER_GROUP)
        h8 = h_ref[pl.ds(base, PEER_GROUP), :]
        g8 = gate_ref[pl.ds(base, PEER_GROUP), :]
        rows = []
        for k in range(PEER_GROUP):
            issue_token(base + ahead_tokens, k, (which + PEER_AHEAD) % PEER_RING)
            gu = jnp.concatenate([buf[k, s] for s in range(n_chunk)], axis=1).astype(jnp.bfloat16)
            z2 = lax.dot_general(_hi_lo_rows(h8[k:k + 1]), gu, (((1,), (1,)), ((), ())),
                                 preferred_element_type=jnp.float32)
            z = z2[0:1] + z2[1:2]
            act = 0.5 * z * (1.0 + lax.erf(z * SQRT_HALF))
            gv = jnp.concatenate([buf[k, n_chunk + s] for s in range(n_chunk)], axis=1).astype(jnp.bfloat16)
            o2 = jnp.dot(_hi_lo_rows(g8[k:k + 1] * act), gv, preferred_element_type=jnp.float32)
            rows.append(o2[0:1] + o2[1:2])
        out_ref[pl.ds(base, PEER_GROUP), :] = jnp.concatenate(rows, axis=0)

    def ring_turn(j, carry):
        for w in range(PEER_RING):
            wait_rows(w)
            compute((j * PEER_RING + w) * PEER_GROUP, w)
        return carry

    lax.fori_loop(0, n_groups // PEER_RING, ring_turn, 0)
    for c in eid_copies(i + 1, nxt):
        c.wait()

    @pl.when(i == nsteps - 1)
    def _():
        for g in range(PEER_AHEAD):
            wait_rows(g)


def _peer_experts(h, eid, gate, uv):
    t = h.shape[0]
    assert t % PEER_TB == 0 and PEER_TB % (PEER_RING * PEER_GROUP) == 0 and 0 < PEER_AHEAD < PEER_RING
    nb = t // PEER_TB
    group_buf = pltpu.VMEM((PEER_GROUP, UV_ROWS, N_PICK, LANE), jnp.float32)
    return pl.pallas_call(
        _peer_expert_body,
        grid=(nb,),
        in_specs=[pl.BlockSpec(memory_space=pl.ANY),
                  pl.BlockSpec((PEER_TB, D_MODEL), lambda i: (i, 0)),
                  pl.BlockSpec((PEER_TB, N_PICK), lambda i: (i, 0)),
                  pl.BlockSpec(memory_space=pl.ANY)],
        out_specs=pl.BlockSpec((PEER_TB, D_MODEL), lambda i: (i, 0)),
        out_shape=jax.ShapeDtypeStruct((t, D_MODEL), jnp.float32),
        scratch_shapes=[pltpu.SMEM((2, PEER_TB + PEER_AHEAD * PEER_GROUP, N_PICK), jnp.int32)]
        + [group_buf] * PEER_RING
        + [pltpu.SemaphoreType.DMA((PEER_RING,)), pltpu.SemaphoreType.DMA],
        compiler_params=pltpu.CompilerParams(dimension_semantics=("arbitrary",),
                                             vmem_limit_bytes=PEER_VMEM_BYTES),
        name="peer_experts",
    )(eid.reshape(nb, PEER_TB, N_PICK), h, gate, uv)


SEL_TB = 256


def _top_rows(s, ids, k):
    n = s.shape[0]
    row = lax.broadcasted_iota(jnp.int32, s.shape, 0).astype(jnp.float32)
    vals, picks = [], []
    for _ in range(k):
        m = jnp.max(s, axis=0, keepdims=True)
        first = jnp.min(jnp.where(s == m, row, float(n)), axis=0, keepdims=True)
        hit = row == first
        vals.append(m)
        picks.append(first if ids is None else jnp.sum(jnp.where(hit, ids, 0.0), axis=0, keepdims=True))
        s = jnp.where(hit, -jnp.inf, s)
    return jnp.concatenate(vals, axis=0), jnp.concatenate(picks, axis=0)


def _peer_select_body(h_ref, wq_ref, keys_ref, eid_ref, gate_ref, q_s, eid_t, gate_t):
    q_s[...] = jnp.dot(h_ref[...].astype(jnp.bfloat16), wq_ref[...], preferred_element_type=jnp.float32)

    def head(hd, carry):
        tops = []
        for p in range(2):
            col = pl.multiple_of((hd * 2 + p) * PEER_HALF, PEER_HALF)
            q_hp = q_s[:, pl.ds(col, PEER_HALF)].astype(jnp.bfloat16)
            s = lax.dot_general(keys_ref[hd * 2 + p], q_hp, (((1,), (1,)), ((), ())),
                                preferred_element_type=jnp.float32)
            tops.append(_top_rows(s, None, PEER_TOPK))
        (v0, i0), (v1, i1) = tops
        cand = jnp.concatenate([v0[i:i + 1] + v1 for i in range(PEER_TOPK)], axis=0)
        cid = jnp.concatenate([i0[i:i + 1] * float(N_KEYS) + i1 for i in range(PEER_TOPK)], axis=0)
        best, eid = _top_rows(cand, cid, PEER_TOPK)
        ex = jnp.exp(best - best[0:1])
        row0 = pl.multiple_of(hd * PEER_TOPK, PEER_TOPK)
        eid_t[pl.ds(row0, PEER_TOPK), :] = eid
        gate_t[pl.ds(row0, PEER_TOPK), :] = ex / jnp.sum(ex, axis=0, keepdims=True)
        return carry

    lax.fori_loop(0, PEER_HEADS, head, 0)
    eid_ref[...] = eid_t[...].T.astype(jnp.int32)
    gate_ref[...] = gate_t[...].T


def _peer_select(h, wq, keys):
    t = h.shape[0]
    assert t % SEL_TB == 0
    qw = PEER_HEADS * PEER_KEY_DIM
    return pl.pallas_call(
        _peer_select_body,
        grid=(t // SEL_TB,),
        in_specs=[pl.BlockSpec((SEL_TB, D_MODEL), lambda i: (i, 0)),
                  pl.BlockSpec((D_MODEL, qw), lambda i: (0, 0)),
                  pl.BlockSpec((PEER_HEADS * 2, N_KEYS, PEER_HALF), lambda i: (0, 0, 0))],
        out_specs=[pl.BlockSpec((SEL_TB, N_PICK), lambda i: (i, 0)),
                   pl.BlockSpec((SEL_TB, N_PICK), lambda i: (i, 0))],
        out_shape=[jax.ShapeDtypeStruct((t, N_PICK), jnp.int32),
                   jax.ShapeDtypeStruct((t, N_PICK), jnp.float32)],
        scratch_shapes=[pltpu.VMEM((SEL_TB, qw), jnp.float32),
                        pltpu.VMEM((N_PICK, SEL_TB), jnp.float32),
                        pltpu.VMEM((N_PICK, SEL_TB), jnp.float32)],
        name="peer_select",
    )(h, wq.astype(jnp.bfloat16), keys.reshape(PEER_HEADS * 2, N_KEYS, PEER_HALF).astype(jnp.bfloat16))


def _peer(h, wq, keys, uv):
    shape = h.shape
    h2 = h.reshape(-1, shape[-1])
    eid, gate = _peer_select(h2, wq, keys)
    return _peer_experts(h2, eid, gate, uv).reshape(shape)


def kernel(x, c, ctx, c_ctx, w_mod, b_mod, ln_w, ln_b, ab_w_in, ab_gate_b, ab_norm_w, ab_sink, ab_w_out,
           gla_w_in, gla_gate_up, gla_gate_b, gla_norm_w, gla_w_out, peer_wq, peer_keys, peer_u, peer_v):
    for layer in range(DEPTH):
        last = layer == DEPTH - 1
        j = layer // 2
        mod_l = jax.nn.silu(c) @ w_mod[layer] + b_mod[layer]
        mod_c = jax.nn.silu(c_ctx) @ w_mod[layer] + b_mod[layer]
        sh1, sc1, g1, sh2, sc2, g2 = jnp.split(mod_l[:, None, :], 6, axis=-1)
        csh1, csc1, cg1, csh2, csc2, cg2 = jnp.split(mod_c, 6)
        hl = x * (1.0 + sc1) + sh1
        hc = ctx * (1.0 + csc1) + csh1
        if layer % 2 == 0:
            y_lat, y_ctx = _mixer_ab(hl, hc, ab_w_in[j], ab_gate_b[j], ab_norm_w[j], ab_sink[j], ab_w_out[j], not last)
        else:
            y_lat, y_ctx = _mixer_c(hl, hc, gla_w_in[j], gla_gate_up[j], gla_gate_b[j], gla_norm_w[j], gla_w_out[j],
                                    not last)
        x = _layernorm(DEEPNORM_ALPHA * x + g1 * y_lat, ln_w[layer, 0], ln_b[layer, 0])
        uv = jnp.concatenate([peer_u[layer].reshape(-1, UV_ROWS // 2, LANE),
                              peer_v[layer].reshape(-1, UV_ROWS // 2, LANE)], axis=1)
        f_lat = _peer(x * (1.0 + sc2) + sh2, peer_wq[layer], peer_keys[layer], uv)
        x = _layernorm(DEEPNORM_ALPHA * x + g2 * f_lat, ln_w[layer, 1], ln_b[layer, 1])
        if not last:
            ctx = _layernorm(DEEPNORM_ALPHA * ctx + cg1 * y_ctx, ln_w[layer, 0], ln_b[layer, 0])
            f_ctx = _peer(ctx * (1.0 + csc2) + csh2, peer_wq[layer], peer_keys[layer], uv)
            ctx = _layernorm(DEEPNORM_ALPHA * ctx + cg2 * f_ctx, ln_w[layer, 1], ln_b[layer, 1])
    return x
```

```python
import functools

import jax
import jax.numpy as jnp
import numpy as np
from jax import lax
from jax.experimental import pallas as pl
from jax.experimental.pallas import tpu as pltpu

D_MODEL = 1024
DEPTH = 2
GRID_W = 64
H_M = 4
DH_M = 128
M_W = H_M * DH_M
H_A = 8
H_KV = 2
DH_A = 64
A_Q = H_A * DH_A
A_KV = H_KV * DH_A
GQA_GROUP = H_A // H_KV
WINDOW = 128
WIN_BLOCK = 128
ROPE_THETA = 10000.0
AB_WIDTHS = (M_W, M_W, M_W, M_W, 4 * H_M, A_Q, A_KV, A_KV)
MIX_W = M_W + A_Q
H_C = 4
DK_C = 128
DV_C = 256
C_K = H_C * DK_C
C_V = H_C * DV_C
GATE_RANK = 16
GATE_TAU = 16.0
C_WIDTHS = (C_K, C_K, C_V, C_V, 2 * GATE_RANK)
CHUNK = 64
N_KEYS = 128
PEER_HEADS = 8
PEER_TOPK = 16
PEER_KEY_DIM = 256
PEER_HALF = PEER_KEY_DIM // 2
PEER_CHUNK = 128
DEEPNORM_ALPHA = (2 * DEPTH) ** 0.25
LN_EPS = 1e-5

LANE = 128
MM_TILE_M = 512
MM_TILE_N = 512


def _matmul_body(x_ref, w_ref, o_ref):
    o_ref[...] = jnp.dot(x_ref[...].astype(jnp.bfloat16), w_ref[...].astype(jnp.bfloat16),
                         preferred_element_type=jnp.float32)


def _matmul(x, w):
    m, k = x.shape
    n = w.shape[1]
    n_pad = -n % MM_TILE_N
    if n_pad:
        w = jnp.pad(w, ((0, 0), (0, n_pad)))
    tm = min(MM_TILE_M, m)
    assert m % tm == 0
    out = pl.pallas_call(
        _matmul_body,
        grid=((n + n_pad) // MM_TILE_N, m // tm),
        in_specs=[pl.BlockSpec((tm, k), lambda j, i: (i, 0)),
                  pl.BlockSpec((k, MM_TILE_N), lambda j, i: (0, j))],
        out_specs=pl.BlockSpec((tm, MM_TILE_N), lambda j, i: (i, j)),
        out_shape=jax.ShapeDtypeStruct((m, n + n_pad), jnp.float32),
        name="proj_matmul",
    )(x, w)
    return out[:, :n] if n_pad else out


def _proj(h, w):
    lead = h.shape[:-1]
    return _matmul(h.reshape(-1, h.shape[-1]), w).reshape(lead + (w.shape[1],))


def _layernorm(x, w, b):
    mu = x.mean(-1, keepdims=True)
    var = jnp.mean(jnp.square(x - mu), -1, keepdims=True)
    return ((x - mu) * lax.rsqrt(var + LN_EPS)) * w + b


def _split(a, widths):
    idx = np.cumsum(widths)[:-1].tolist()
    return jnp.split(a, idx, axis=-1)


def _heads(a, h):
    b, s = a.shape[:2]
    return a.reshape(b, s, h, -1).transpose(0, 2, 1, 3)


def _head_norm(h, w):
    hf = h * lax.rsqrt(jnp.mean(h * h, -1, keepdims=True) + LN_EPS)
    b, nh, s, d = h.shape
    return hf.transpose(0, 2, 1, 3).reshape(b, s, nh * d) * w


def _to_chunks(a):
    b, h, s = a.shape[:3]
    a = a.reshape((b, h, s // CHUNK, CHUNK) + a.shape[3:])
    return jnp.moveaxis(a, 2, 0)


def _from_chunks(a):
    a = jnp.moveaxis(a, 0, 2)
    return a.reshape(a.shape[:2] + (-1,) + a.shape[4:])


def _mlstm_scan(q, k, v, log_i, log_f, state):
    causal = jnp.tril(jnp.ones((CHUNK, CHUNK), bool))

    def step(carry, inp):
        c_st, n_st, m_st = carry
        qc, kc, vc, li, lf = inp
        cum = jnp.cumsum(lf, axis=-1)
        dmat = cum[..., :, None] - cum[..., None, :] + li[..., None, :]
        dmat = jnp.where(causal, dmat, -jnp.inf)
        m_inter = cum + m_st[..., None]
        m_t = jnp.maximum(m_inter, dmat.max(-1))
        w = jnp.exp(dmat - m_t[..., None]) * jnp.einsum('bhtd,bhsd->bhts', qc, kc)
        a_inter = jnp.exp(m_inter - m_t)
        num = a_inter[..., None] * jnp.einsum('bhtd,bhde->bhte', qc, c_st) + jnp.einsum('bhts,bhse->bhte', w, vc)
        den = a_inter * jnp.einsum('bhtd,bhd->bht', qc, n_st) + w.sum(-1)
        h = num / jnp.maximum(jnp.abs(den), jnp.exp(-m_t))[..., None]
        total = cum[..., -1]
        decay_s = total[..., None] - cum + li
        m_new = jnp.maximum(total + m_st, decay_s.max(-1))
        ws = jnp.exp(decay_s - m_new[..., None])
        a_st = jnp.exp(total + m_st - m_new)
        c_st = a_st[..., None, None] * c_st + jnp.einsum('bhs,bhsd,bhse->bhde', ws, kc, vc)
        n_st = a_st[..., None] * n_st + jnp.einsum('bhs,bhsd->bhd', ws, kc)
        return (c_st, n_st, m_new), h

    state, hs = lax.scan(step, state, tuple(_to_chunks(a) for a in (q, k, v, log_i, log_f)))
    return _from_chunks(hs), state


def _gla_scan(q, k, v, log_a, state):
    causal = jnp.tril(jnp.ones((CHUNK, CHUNK), bool))

    def step(st, inp):
        qc, kc, vc, la = inp
        cum = jnp.cumsum(la, axis=2)
        rel = cum[:, :, :, None, :] - cum[:, :, None, :, :]
        rel = jnp.where(causal[:, :, None], rel, -jnp.inf)
        scores = jnp.einsum('bhtk,bhsk,bhtsk->bhts', qc, kc, jnp.exp(rel))
        out = jnp.einsum('bhtk,bhkv->bhtv', qc * jnp.exp(cum), st) + jnp.einsum('bhts,bhsv->bhtv', scores, vc)
        total = cum[:, :, -1:, :]
        st = jnp.exp(total[:, :, 0, :, None]) * st + jnp.einsum('bhsk,bhsv->bhkv', kc * jnp.exp(total - cum), vc)
        return st, out

    state, outs = lax.scan(step, state, tuple(_to_chunks(a) for a in (q, k, v, log_a)))
    return _from_chunks(outs), state


def _directional(scan_fn, ctx_seq, lat_seq, init, reverse):
    if reverse:
        ctx_seq = tuple(jnp.flip(a, 2) for a in ctx_seq)
        lat_seq = tuple(jnp.flip(a, 2) for a in lat_seq)
    h_ctx, state = scan_fn(*ctx_seq, init)
    h_lat, _ = scan_fn(*lat_seq, state)
    if reverse:
        h_ctx, h_lat = jnp.flip(h_ctx, 2), jnp.flip(h_lat, 2)
    return h_ctx, h_lat


def _axial_rope(s):
    rows = s // GRID_W
    row = jnp.repeat(jnp.arange(rows), GRID_W).astype(jnp.float32)
    col = jnp.tile(jnp.arange(GRID_W), rows).astype(jnp.float32)
    n_freq = DH_A // 4
    inv = ROPE_THETA ** (-jnp.arange(n_freq, dtype=jnp.float32) / n_freq)
    ang = jnp.concatenate([row[:, None] * inv, col[:, None] * inv], -1)
    return jnp.cos(ang), jnp.sin(ang)


def _rope(x, cos, sin):
    x1, x2 = x[..., 0::2], x[..., 1::2]
    c = cos[None, :, None, :]
    sn = sin[None, :, None, :]
    return jnp.stack([x1 * c - x2 * sn, x1 * sn + x2 * c], -1).reshape(x.shape)


def _window_attention(q, k, v, k_ctx, v_ctx, sink):
    b, s = q.shape[:2]
    nb = s // WIN_BLOCK
    lc = k_ctx.shape[1]
    qb = jnp.moveaxis(q.reshape(b, nb, WIN_BLOCK, H_KV, GQA_GROUP, DH_A), 1, 0)

    def bands(a):
        ap = jnp.pad(a, ((0, 0), (WIN_BLOCK, WIN_BLOCK), (0, 0), (0, 0))).reshape(b, nb + 2, WIN_BLOCK, H_KV, DH_A)
        return jnp.moveaxis(jnp.concatenate([ap[:, :-2], ap[:, 1:-1], ap[:, 2:]], 2), 1, 0)

    kw, vw = bands(k), bands(v)
    qi = jnp.arange(WIN_BLOCK)[:, None]
    kj = jnp.arange(3 * WIN_BLOCK)[None, :]
    key_pos = jnp.arange(nb)[:, None, None] * WIN_BLOCK + kj - WIN_BLOCK
    valid = (jnp.abs(kj - WIN_BLOCK - qi) <= WINDOW) & (key_pos >= 0) & (key_pos < s)
    scale = DH_A ** -0.5
    sink_l = sink.reshape(H_KV, GQA_GROUP)

    def block(args):
        qn, kn, vn, mask = args
        s_ctx = jnp.einsum('bqhgd,bchd->bhgqc', qn, k_ctx) * scale
        s_win = jnp.einsum('bqhgd,bkhd->bhgqk', qn, kn) * scale
        s_win = jnp.where(mask, s_win, -jnp.inf)
        sink_col = jnp.broadcast_to(sink_l[None, :, :, None, None], s_ctx.shape[:-1] + (1,))
        p = jax.nn.softmax(jnp.concatenate([sink_col, s_ctx, s_win], -1), -1)
        return (jnp.einsum('bhgqc,bchd->bqhgd', p[..., 1:1 + lc], v_ctx)
                + jnp.einsum('bhgqk,bkhd->bqhgd', p[..., 1 + lc:], vn))

    out = lax.map(block, (qb, kw, vw, valid))
    return jnp.moveaxis(out, 0, 1).reshape(b, s, H_A, DH_A)


def _context_attention(q, k, v, sink):
    b, lc = q.shape[:2]
    qg = q.reshape(b, lc, H_KV, GQA_GROUP, DH_A)
    sc = jnp.einsum('bqhgd,bkhd->bhgqk', qg, k) * DH_A ** -0.5
    sink_col = jnp.broadcast_to(sink.reshape(H_KV, GQA_GROUP)[None, :, :, None, None], sc.shape[:-1] + (1,))
    p = jax.nn.softmax(jnp.concatenate([sink_col, sc], -1), -1)[..., 1:]
    return jnp.einsum('bhgqk,bkhd->bqhgd', p, v).reshape(b, lc, H_A, DH_A)


def _ab_streams(h, w_in, gate_b):
    q_m, k_m, v_m, o_m, g_m, q_a, k_a, v_a = _split(_proj(h, w_in), AB_WIDTHS)
    b, s = h.shape[:2]
    g = g_m.reshape(b, s, 2, 2, H_M) + gate_b
    mlstm = (_heads(q_m, H_M), _heads(k_m, H_M) * DH_M ** -0.5, _heads(v_m, H_M))
    gates = [(g[:, :, d, 0].transpose(0, 2, 1), jax.nn.log_sigmoid(g[:, :, d, 1]).transpose(0, 2, 1))
             for d in range(2)]
    attn = (q_a.reshape(b, s, H_A, DH_A), k_a.reshape(b, s, H_KV, DH_A), v_a.reshape(b, s, H_KV, DH_A))
    return mlstm, gates, o_m, attn


def _merge_ab(m, o, a, norm_w, w_out):
    hm = _head_norm(m, norm_w) * jax.nn.sigmoid(o)
    cat = jnp.concatenate([hm, a.reshape(a.shape[0], a.shape[1], A_Q)], -1)
    return _proj(cat, w_out)


def _mixer_ab(hl, hc, w_in, gate_b, norm_w, sink, w_out, ctx_out):
    ml, gl, ol, (ql, kl, vl) = _ab_streams(hl, w_in, gate_b)
    mc, gc, oc, (qc, kc, vc) = _ab_streams(hc, w_in, gate_b)
    b = hl.shape[0]
    init = (jnp.zeros((b, H_M, DH_M, DH_M), jnp.float32), jnp.zeros((b, H_M, DH_M), jnp.float32),
            jnp.zeros((b, H_M), jnp.float32))
    outs = [_directional(_mlstm_scan, mc + gc[d], ml + gl[d], init, d == 1) for d in range(2)]
    cos, sin = _axial_rope(hl.shape[1])
    a_lat = _window_attention(_rope(ql, cos, sin), _rope(kl, cos, sin), vl, kc, vc, sink)
    y_lat = _merge_ab(outs[0][1] + outs[1][1], ol, a_lat, norm_w, w_out)
    if not ctx_out:
        return y_lat, None
    y_ctx = _merge_ab(outs[0][0] + outs[1][0], oc, _context_attention(qc, kc, vc, sink), norm_w, w_out)
    return y_lat, y_ctx


def _c_streams(h, w_in, gate_up, gate_b):
    q, k, v, g, low = _split(_proj(h, w_in), C_WIDTHS)
    b, s = h.shape[:2]
    qkv = (_heads(q, H_C) * DK_C ** -0.5, _heads(k, H_C), _heads(v, H_C))
    low = low.reshape(b, s, 2, GATE_RANK)
    log_a = [_heads(jax.nn.log_sigmoid(low[:, :, d] @ gate_up[d] + gate_b[d]) / GATE_TAU, H_C) for d in range(2)]
    return qkv, log_a, g


def _merge_c(h, g, norm_w, w_out):
    return _proj(_head_norm(h, norm_w) * jax.nn.silu(g), w_out)


GLA_SUB = 16


def _gla_body(q_ref, k_ref, v_ref, cum_ref, o_ref, st_t):
    @pl.when(pl.program_id(1) == 0)
    def _():
        st_t[...] = jnp.zeros_like(st_t)

    bf = jnp.bfloat16
    nt = (((1,), (1,)), ((), ()))
    q, k, v, cum = q_ref[0], k_ref[0], v_ref[0], cum_ref[0]
    vb = v.astype(bf)
    inter = lax.dot_general((q * jnp.exp(cum)).astype(bf), st_t[...].astype(bf), nt,
                            preferred_element_type=jnp.float32)
    row = lax.broadcasted_iota(jnp.int32, (GLA_SUB, DK_C), 0)
    blocks = []
    for i in range(CHUNK // GLA_SUB):
        lo = i * GLA_SUB
        qi, ki, vi, ci = q[lo:lo + GLA_SUB], k[lo:lo + GLA_SUB], v[lo:lo + GLA_SUB], cum[lo:lo + GLA_SUB]
        acc = inter[lo:lo + GLA_SUB]
        if i > 0:
            p = cum[lo:lo + 1]
            a = (qi * jnp.exp(ci - p)).astype(bf)
            b = (k[:lo] * jnp.exp(p - cum[:lo])).astype(bf)
            sc = lax.dot_general(a, b, nt, preferred_element_type=jnp.float32)
            acc = acc + jnp.dot(sc.astype(bf), vb[:lo], preferred_element_type=jnp.float32)
        for s in range(GLA_SUB):
            e = jnp.where(row >= s, jnp.exp(ci - ci[s:s + 1]), 0.0)
            col = jnp.sum(qi * ki[s:s + 1] * e, axis=1, keepdims=True)
            acc = acc + col * vi[s:s + 1]
        blocks.append(acc)
    o_ref[0] = jnp.concatenate(blocks, axis=0)

    total = cum[CHUNK - 1:CHUNK]
    kd = k * jnp.exp(total - cum)
    pad = jnp.zeros((LANE - CHUNK, DK_C), jnp.float32)
    vpad = jnp.zeros((LANE - CHUNK, DV_C), jnp.float32)
    v_t = jnp.concatenate([v, vpad], axis=0).T
    st_t[...] = st_t[...] * jnp.exp(total) + jnp.dot(v_t.astype(bf), jnp.concatenate([kd, pad], axis=0).astype(bf),
                                                      preferred_element_type=jnp.float32)


def _gla_chunks(q, k, v, cum):
    n, s = q.shape[:2]
    assert s % CHUNK == 0
    kspec = pl.BlockSpec((1, CHUNK, DK_C), lambda i, c: (i, c, 0))
    vspec = pl.BlockSpec((1, CHUNK, DV_C), lambda i, c: (i, c, 0))
    return pl.pallas_call(
        _gla_body,
        grid=(n, s // CHUNK),
        in_specs=[kspec, kspec, vspec, kspec],
        out_specs=vspec,
        out_shape=jax.ShapeDtypeStruct((n, s, DV_C), jnp.float32),
        scratch_shapes=[pltpu.VMEM((DV_C, DK_C), jnp.float32)],
        compiler_params=pltpu.CompilerParams(dimension_semantics=("arbitrary", "arbitrary")),
        name="gla_chunks",
    )(q, k, v, cum)


def _gla_bidirectional(sc, ac, sl, al):
    lc = sc[0].shape[2]

    def seq(ctx_a, lat_a, reverse):
        if reverse:
            ctx_a, lat_a = jnp.flip(ctx_a, 2), jnp.flip(lat_a, 2)
        a = jnp.concatenate([ctx_a, lat_a], axis=2)
        return a.reshape((-1,) + a.shape[2:])

    def chunk_cumsum(a):
        n, s, d = a.shape
        return jnp.cumsum(a.reshape(n, s // CHUNK, CHUNK, d), axis=2).reshape(n, s, d)

    q, k, v = (jnp.concatenate([seq(c_, l_, False), seq(c_, l_, True)], axis=0) for c_, l_ in zip(sc, sl))
    cum = chunk_cumsum(jnp.concatenate([seq(ac[0], al[0], False), seq(ac[1], al[1], True)], axis=0))
    out = _gla_chunks(q, k, v, cum).reshape((2,) + sl[2].shape[:2] + (-1, DV_C))
    fwd, bwd = out[0], out[1]
    h_ctx = fwd[:, :, :lc] + jnp.flip(bwd[:, :, :lc], 2)
    h_lat = fwd[:, :, lc:] + jnp.flip(bwd[:, :, lc:], 2)
    return h_ctx, h_lat


def _mixer_c(hl, hc, w_in, gate_up, gate_b, norm_w, w_out, ctx_out):
    sl, al, gl = _c_streams(hl, w_in, gate_up, gate_b)
    sc, ac, gc = _c_streams(hc, w_in, gate_up, gate_b)
    h_ctx, h_lat = _gla_bidirectional(sc, ac, sl, al)
    y_lat = _merge_c(h_lat, gl, norm_w, w_out)
    if not ctx_out:
        return y_lat, None
    return y_lat, _merge_c(h_ctx, gc, norm_w, w_out)


PEER_GROUP = 8
PEER_RING = 4
PEER_AHEAD = 3
PEER_TB = 256
DMA_QUEUES = 2
PEER_VMEM_BYTES = (PEER_RING * PEER_GROUP * 2 * D_MODEL * PEER_HEADS * PEER_TOPK * 4
                   + 2 * PEER_TB * (2 * D_MODEL + PEER_HEADS * PEER_TOPK) * 4 + (8 << 20))
N_PICK = PEER_HEADS * PEER_TOPK
MXU_ROWS = 16
SQRT_HALF = 0.7071067811865476
UV_ROWS = 2 * D_MODEL // LANE


def _hi_lo_rows(row):
    n = row.shape[1]
    hi = row.astype(jnp.bfloat16).astype(jnp.float32)
    lo = row - hi
    sub = lax.broadcasted_iota(jnp.int32, (MXU_ROWS, n), 0)
    full = jnp.where(sub == 0, jnp.broadcast_to(hi, (MXU_ROWS, n)),
                     jnp.where(sub == 1, jnp.broadcast_to(lo, (MXU_ROWS, n)), 0.0))
    return full.astype(jnp.bfloat16)


def _peer_expert_body(eid_hbm, h_ref, gate_ref, uv_hbm, out_ref, eid_s, *rest):
    bufs, (row_sems, eid_sem) = rest[:PEER_RING], rest[PEER_RING:]
    i = pl.program_id(0)
    nsteps = pl.num_programs(0)
    cur = i % 2
    nxt = 1 - cur
    n_groups = PEER_TB // PEER_GROUP
    ahead_tokens = PEER_AHEAD * PEER_GROUP

    def eid_copies(step, slot):
        blk = jnp.minimum(step, nsteps - 1)
        tail = jnp.minimum(step + 1, nsteps - 1)
        return (pltpu.make_async_copy(eid_hbm.at[blk], eid_s.at[slot, pl.ds(0, PEER_TB)], eid_sem),
                pltpu.make_async_copy(eid_hbm.at[tail, pl.ds(0, ahead_tokens)],
                                      eid_s.at[slot, pl.ds(PEER_TB, ahead_tokens)], eid_sem))

    def issue_token(base, k, which):
        for e in range(N_PICK):
            pltpu.make_async_copy(uv_hbm.at[eid_s[cur, base + k, e]], bufs[which].at[k, :, e, :],
                                  row_sems.at[which]).start(priority=e % DMA_QUEUES)

    def issue(base, which):
        for k in range(PEER_GROUP):
            issue_token(base, k, which)

    def wait_rows(which):
        pltpu.make_async_copy(bufs[which], bufs[which], row_sems.at[which]).wait()

    @pl.when(i == 0)
    def _():
        for c in eid_copies(0, 0):
            c.start()
        for c in eid_copies(0, 0):
            c.wait()
        for g in range(PEER_AHEAD):
            issue(g * PEER_GROUP, g)

    for c in eid_copies(i + 1, nxt):
        c.start()

    n_chunk = D_MODEL // LANE

    def compute(base, which):
        buf = bufs[which]
        base = pl.multiple_of(base, PEER_GROUP)
        h8 = h_ref[pl.ds(base, PEER_GROUP), :]
        g8 = gate_ref[pl.ds(base, PEER_GROUP), :]
        rows = []
        for k in range(PEER_GROUP):
            issue_token(base + ahead_tokens, k, (which + PEER_AHEAD) % PEER_RING)
            gu = jnp.concatenate([buf[k, s] for s in range(n_chunk)], axis=1).astype(jnp.bfloat16)
            z2 = lax.dot_general(_hi_lo_rows(h8[k:k + 1]), gu, (((1,), (1,)), ((), ())),
                                 preferred_element_type=jnp.float32)
            z = z2[0:1] + z2[1:2]
            act = 0.5 * z * (1.0 + lax.erf(z * SQRT_HALF))
            gv = jnp.concatenate([buf[k, n_chunk + s] for s in range(n_chunk)], axis=1).astype(jnp.bfloat16)
            o2 = jnp.dot(_hi_lo_rows(g8[k:k + 1] * act), gv, preferred_element_type=jnp.float32)
            rows.append(o2[0:1] + o2[1:2])
        out_ref[pl.ds(base, PEER_GROUP), :] = jnp.concatenate(rows, axis=0)

    def ring_turn(j, carry):
        for w in range(PEER_RING):
            wait_rows(w)
            compute((j * PEER_RING + w) * PEER_GROUP, w)
        return carry

    lax.fori_loop(0, n_groups // PEER_RING, ring_turn, 0)
    for c in eid_copies(i + 1, nxt):
        c.wait()

    @pl.when(i == nsteps - 1)
    def _():
        for g in range(PEER_AHEAD):
            wait_rows(g)


def _peer_experts(h, eid, gate, uv):
    t = h.shape[0]
    assert t % PEER_TB == 0 and PEER_TB % (PEER_RING * PEER_GROUP) == 0 and 0 < PEER_AHEAD < PEER_RING
    nb = t // PEER_TB
    group_buf = pltpu.VMEM((PEER_GROUP, UV_ROWS, N_PICK, LANE), jnp.float32)
    return pl.pallas_call(
        _peer_expert_body,
        grid=(nb,),
        in_specs=[pl.BlockSpec(memory_space=pl.ANY),
                  pl.BlockSpec((PEER_TB, D_MODEL), lambda i: (i, 0)),
                  pl.BlockSpec((PEER_TB, N_PICK), lambda i: (i, 0)),
                  pl.BlockSpec(memory_space=pl.ANY)],
        out_specs=pl.BlockSpec((PEER_TB, D_MODEL), lambda i: (i, 0)),
        out_shape=jax.ShapeDtypeStruct((t, D_MODEL), jnp.float32),
        scratch_shapes=[pltpu.SMEM((2, PEER_TB + PEER_AHEAD * PEER_GROUP, N_PICK), jnp.int32)]
        + [group_buf] * PEER_RING
        + [pltpu.SemaphoreType.DMA((PEER_RING,)), pltpu.SemaphoreType.DMA],
        compiler_params=pltpu.CompilerParams(dimension_semantics=("arbitrary",),
                                             vmem_limit_bytes=PEER_VMEM_BYTES),
        name="peer_experts",
    )(eid.reshape(nb, PEER_TB, N_PICK), h, gate, uv)


SEL_TB = 256


def _top_rows(s, ids, k):
    n = s.shape[0]
    row = lax.broadcasted_iota(jnp.int32, s.shape, 0).astype(jnp.float32)
    vals, picks = [], []
    for _ in range(k):
        m = jnp.max(s, axis=0, keepdims=True)
        first = jnp.min(jnp.where(s == m, row, float(n)), axis=0, keepdims=True)
        hit = row == first
        vals.append(m)
        picks.append(first if ids is None else jnp.sum(jnp.where(hit, ids, 0.0), axis=0, keepdims=True))
        s = jnp.where(hit, -jnp.inf, s)
    return jnp.concatenate(vals, axis=0), jnp.concatenate(picks, axis=0)


def _peer_select_body(h_ref, wq_ref, keys_ref, eid_ref, gate_ref, q_s, eid_t, gate_t):
    q_s[...] = jnp.dot(h_ref[...].astype(jnp.bfloat16), wq_ref[...], preferred_element_type=jnp.float32)

    def head(hd, carry):
        tops = []
        for p in range(2):
            col = pl.multiple_of((hd * 2 + p) * PEER_HALF, PEER_HALF)
            q_hp = q_s[:, pl.ds(col, PEER_HALF)].astype(jnp.bfloat16)
            s = lax.dot_general(keys_ref[hd * 2 + p], q_hp, (((1,), (1,)), ((), ())),
                                preferred_element_type=jnp.float32)
            tops.append(_top_rows(s, None, PEER_TOPK))
        (v0, i0), (v1, i1) = tops
        cand = jnp.concatenate([v0[i:i + 1] + v1 for i in range(PEER_TOPK)], axis=0)
        cid = jnp.concatenate([i0[i:i + 1] * float(N_KEYS) + i1 for i in range(PEER_TOPK)], axis=0)
        best, eid = _top_rows(cand, cid, PEER_TOPK)
        ex = jnp.exp(best - best[0:1])
        row0 = pl.multiple_of(hd * PEER_TOPK, PEER_TOPK)
        eid_t[pl.ds(row0, PEER_TOPK), :] = eid
        gate_t[pl.ds(row0, PEER_TOPK), :] = ex / jnp.sum(ex, axis=0, keepdims=True)
        return carry

    lax.fori_loop(0, PEER_HEADS, head, 0)
    eid_ref[...] = eid_t[...].T.astype(jnp.int32)
    gate_ref[...] = gate_t[...].T


def _peer_select(h, wq, keys):
    t = h.shape[0]
    assert t % SEL_TB == 0
    qw = PEER_HEADS * PEER_KEY_DIM
    return pl.pallas_call(
        _peer_select_body,
        grid=(t // SEL_TB,),
        in_specs=[pl.BlockSpec((SEL_TB, D_MODEL), lambda i: (i, 0)),
                  pl.BlockSpec((D_MODEL, qw), lambda i: (0, 0)),
                  pl.BlockSpec((PEER_HEADS * 2, N_KEYS, PEER_HALF), lambda i: (0, 0, 0))],
        out_specs=[pl.BlockSpec((SEL_TB, N_PICK), lambda i: (i, 0)),
                   pl.BlockSpec((SEL_TB, N_PICK), lambda i: (i, 0))],
        out_shape=[jax.ShapeDtypeStruct((t, N_PICK), jnp.int32),
                   jax.ShapeDtypeStruct((t, N_PICK), jnp.float32)],
        scratch_shapes=[pltpu.VMEM((SEL_TB, qw), jnp.float32),
                        pltpu.VMEM((N_PICK, SEL_TB), jnp.float32),
                        pltpu.VMEM((N_PICK, SEL_TB), jnp.float32)],
        name="peer_select",
    )(h, wq.astype(jnp.bfloat16), keys.reshape(PEER_HEADS * 2, N_KEYS, PEER_HALF).astype(jnp.bfloat16))


def _peer(h, wq, keys, uv):
    shape = h.shape
    h2 = h.reshape(-1, shape[-1])
    eid, gate = _peer_select(h2, wq, keys)
    return _peer_experts(h2, eid, gate, uv).reshape(shape)


def kernel(x, c, ctx, c_ctx, w_mod, b_mod, ln_w, ln_b, ab_w_in, ab_gate_b, ab_norm_w, ab_sink, ab_w_out,
           gla_w_in, gla_gate_up, gla_gate_b, gla_norm_w, gla_w_out, peer_wq, peer_keys, peer_u, peer_v):
    for layer in range(DEPTH):
        last = layer == DEPTH - 1
        j = layer // 2
        mod_l = jax.nn.silu(c) @ w_mod[layer] + b_mod[layer]
        mod_c = jax.nn.silu(c_ctx) @ w_mod[layer] + b_mod[layer]
        sh1, sc1, g1, sh2, sc2, g2 = jnp.split(mod_l[:, None, :], 6, axis=-1)
        csh1, csc1, cg1, csh2, csc2, cg2 = jnp.split(mod_c, 6)
        hl = x * (1.0 + sc1) + sh1
        hc = ctx * (1.0 + csc1) + csh1
        if layer % 2 == 0:
            y_lat, y_ctx = _mixer_ab(hl, hc, ab_w_in[j], ab_gate_b[j], ab_norm_w[j], ab_sink[j], ab_w_out[j], not last)
        else:
            y_lat, y_ctx = _mixer_c(hl, hc, gla_w_in[j], gla_gate_up[j], gla_gate_b[j], gla_norm_w[j], gla_w_out[j],
                                    not last)
        x = _layernorm(DEEPNORM_ALPHA * x + g1 * y_lat, ln_w[layer, 0], ln_b[layer, 0])
        uv = jnp.concatenate([peer_u[layer].reshape(-1, UV_ROWS // 2, LANE),
                              peer_v[layer].reshape(-1, UV_ROWS // 2, LANE)], axis=1)
        f_lat = _peer(x * (1.0 + sc2) + sh2, peer_wq[layer], peer_keys[layer], uv)
        x = _layernorm(DEEPNORM_ALPHA * x + g2 * f_lat, ln_w[layer, 1], ln_b[layer, 1])
        if not last:
            ctx = _layernorm(DEEPNORM_ALPHA * ctx + cg1 * y_ctx, ln_w[layer, 0], ln_b[layer, 0])
            f_ctx = _peer(ctx * (1.0 + csc2) + csh2, peer_wq[layer], peer_keys[layer], uv)
            ctx = _layernorm(DEEPNORM_ALPHA * ctx + cg2 * f_ctx, ln_w[layer, 1], ln_b[layer, 1])
    return x
```

```python
import functools

import jax
import jax.numpy as jnp
import numpy as np
from jax import lax
from jax.experimental import pallas as pl
from jax.experimental.pallas import tpu as pltpu

D_MODEL = 1024
DEPTH = 2
GRID_W = 64
H_M = 4
DH_M = 128
M_W = H_M * DH_M
H_A = 8
H_KV = 2
DH_A = 64
A_Q = H_A * DH_A
A_KV = H_KV * DH_A
GQA_GROUP = H_A // H_KV
WINDOW = 128
WIN_BLOCK = 128
ROPE_THETA = 10000.0
AB_WIDTHS = (M_W, M_W, M_W, M_W, 4 * H_M, A_Q, A_KV, A_KV)
MIX_W = M_W + A_Q
H_C = 4
DK_C = 128
DV_C = 256
C_K = H_C * DK_C
C_V = H_C * DV_C
GATE_RANK = 16
GATE_TAU = 16.0
C_WIDTHS = (C_K, C_K, C_V, C_V, 2 * GATE_RANK)
CHUNK = 64
N_KEYS = 128
PEER_HEADS = 8
PEER_TOPK = 16
PEER_KEY_DIM = 256
PEER_HALF = PEER_KEY_DIM // 2
PEER_CHUNK = 128
DEEPNORM_ALPHA = (2 * DEPTH) ** 0.25
LN_EPS = 1e-5

LANE = 128
MM_TILE_M = 512
MM_TILE_N = 512


def _matmul_body(x_ref, w_ref, o_ref):
    o_ref[...] = jnp.dot(x_ref[...].astype(jnp.bfloat16), w_ref[...].astype(jnp.bfloat16),
                         preferred_element_type=jnp.float32)


def _matmul(x, w):
    m, k = x.shape
    n = w.shape[1]
    n_pad = -n % MM_TILE_N
    if n_pad:
        w = jnp.pad(w, ((0, 0), (0, n_pad)))
    tm = min(MM_TILE_M, m)
    assert m % tm == 0
    out = pl.pallas_call(
        _matmul_body,
        grid=((n + n_pad) // MM_TILE_N, m // tm),
        in_specs=[pl.BlockSpec((tm, k), lambda j, i: (i, 0)),
                  pl.BlockSpec((k, MM_TILE_N), lambda j, i: (0, j))],
        out_specs=pl.BlockSpec((tm, MM_TILE_N), lambda j, i: (i, j)),
        out_shape=jax.ShapeDtypeStruct((m, n + n_pad), jnp.float32),
        name="proj_matmul",
    )(x, w)
    return out[:, :n] if n_pad else out


def _proj(h, w):
    lead = h.shape[:-1]
    return _matmul(h.reshape(-1, h.shape[-1]), w).reshape(lead + (w.shape[1],))


def _layernorm(x, w, b):
    mu = x.mean(-1, keepdims=True)
    var = jnp.mean(jnp.square(x - mu), -1, keepdims=True)
    return ((x - mu) * lax.rsqrt(var + LN_EPS)) * w + b


def _split(a, widths):
    idx = np.cumsum(widths)[:-1].tolist()
    return jnp.split(a, idx, axis=-1)


def _heads(a, h):
    b, s = a.shape[:2]
    return a.reshape(b, s, h, -1).transpose(0, 2, 1, 3)


def _head_norm(h, w):
    hf = h * lax.rsqrt(jnp.mean(h * h, -1, keepdims=True) + LN_EPS)
    b, nh, s, d = h.shape
    return hf.transpose(0, 2, 1, 3).reshape(b, s, nh * d) * w


def _to_chunks(a):
    b, h, s = a.shape[:3]
    a = a.reshape((b, h, s // CHUNK, CHUNK) + a.shape[3:])
    return jnp.moveaxis(a, 2, 0)


def _from_chunks(a):
    a = jnp.moveaxis(a, 0, 2)
    return a.reshape(a.shape[:2] + (-1,) + a.shape[4:])


def _mlstm_scan(q, k, v, log_i, log_f, state):
    causal = jnp.tril(jnp.ones((CHUNK, CHUNK), bool))

    def step(carry, inp):
        c_st, n_st, m_st = carry
        qc, kc, vc, li, lf = inp
        cum = jnp.cumsum(lf, axis=-1)
        dmat = cum[..., :, None] - cum[..., None, :] + li[..., None, :]
        dmat = jnp.where(causal, dmat, -jnp.inf)
        m_inter = cum + m_st[..., None]
        m_t = jnp.maximum(m_inter, dmat.max(-1))
        w = jnp.exp(dmat - m_t[..., None]) * jnp.einsum('bhtd,bhsd->bhts', qc, kc)
        a_inter = jnp.exp(m_inter - m_t)
        num = a_inter[..., None] * jnp.einsum('bhtd,bhde->bhte', qc, c_st) + jnp.einsum('bhts,bhse->bhte', w, vc)
        den = a_inter * jnp.einsum('bhtd,bhd->bht', qc, n_st) + w.sum(-1)
        h = num / jnp.maximum(jnp.abs(den), jnp.exp(-m_t))[..., None]
        total = cum[..., -1]
        decay_s = total[..., None] - cum + li
        m_new = jnp.maximum(total + m_st, decay_s.max(-1))
        ws = jnp.exp(decay_s - m_new[..., None])
        a_st = jnp.exp(total + m_st - m_new)
        c_st = a_st[..., None, None] * c_st + jnp.einsum('bhs,bhsd,bhse->bhde', ws, kc, vc)
        n_st = a_st[..., None] * n_st + jnp.einsum('bhs,bhsd->bhd', ws, kc)
        return (c_st, n_st, m_new), h

    state, hs = lax.scan(step, state, tuple(_to_chunks(a) for a in (q, k, v, log_i, log_f)))
    return _from_chunks(hs), state


def _gla_scan(q, k, v, log_a, state):
    causal = jnp.tril(jnp.ones((CHUNK, CHUNK), bool))

    def step(st, inp):
        qc, kc, vc, la = inp
        cum = jnp.cumsum(la, axis=2)
        rel = cum[:, :, :, None, :] - cum[:, :, None, :, :]
        rel = jnp.where(causal[:, :, None], rel, -jnp.inf)
        scores = jnp.einsum('bhtk,bhsk,bhtsk->bhts', qc, kc, jnp.exp(rel))
        out = jnp.einsum('bhtk,bhkv->bhtv', qc * jnp.exp(cum), st) + jnp.einsum('bhts,bhsv->bhtv', scores, vc)
        total = cum[:, :, -1:, :]
        st = jnp.exp(total[:, :, 0, :, None]) * st + jnp.einsum('bhsk,bhsv->bhkv', kc * jnp.exp(total - cum), vc)
        return st, out

    state, outs = lax.scan(step, state, tuple(_to_chunks(a) for a in (q, k, v, log_a)))
    return _from_chunks(outs), state


def _directional(scan_fn, ctx_seq, lat_seq, init, reverse):
    if reverse:
        ctx_seq = tuple(jnp.flip(a, 2) for a in ctx_seq)
        lat_seq = tuple(jnp.flip(a, 2) for a in lat_seq)
    h_ctx, state = scan_fn(*ctx_seq, init)
    h_lat, _ = scan_fn(*lat_seq, state)
    if reverse:
        h_ctx, h_lat = jnp.flip(h_ctx, 2), jnp.flip(h_lat, 2)
    return h_ctx, h_lat


def _axial_rope(s):
    rows = s // GRID_W
    row = jnp.repeat(jnp.arange(rows), GRID_W).astype(jnp.float32)
    col = jnp.tile(jnp.arange(GRID_W), rows).astype(jnp.float32)
    n_freq = DH_A // 4
    inv = ROPE_THETA ** (-jnp.arange(n_freq, dtype=jnp.float32) / n_freq)
    ang = jnp.concatenate([row[:, None] * inv, col[:, None] * inv], -1)
    return jnp.cos(ang), jnp.sin(ang)


def _rope(x, cos, sin):
    x1, x2 = x[..., 0::2], x[..., 1::2]
    c = cos[None, :, None, :]
    sn = sin[None, :, None, :]
    return jnp.stack([x1 * c - x2 * sn, x1 * sn + x2 * c], -1).reshape(x.shape)


def _window_attention(q, k, v, k_ctx, v_ctx, sink):
    b, s = q.shape[:2]
    nb = s // WIN_BLOCK
    lc = k_ctx.shape[1]
    qb = jnp.moveaxis(q.reshape(b, nb, WIN_BLOCK, H_KV, GQA_GROUP, DH_A), 1, 0)

    def bands(a):
        ap = jnp.pad(a, ((0, 0), (WIN_BLOCK, WIN_BLOCK), (0, 0), (0, 0))).reshape(b, nb + 2, WIN_BLOCK, H_KV, DH_A)
        return jnp.moveaxis(jnp.concatenate([ap[:, :-2], ap[:, 1:-1], ap[:, 2:]], 2), 1, 0)

    kw, vw = bands(k), bands(v)
    qi = jnp.arange(WIN_BLOCK)[:, None]
    kj = jnp.arange(3 * WIN_BLOCK)[None, :]
    key_pos = jnp.arange(nb)[:, None, None] * WIN_BLOCK + kj - WIN_BLOCK
    valid = (jnp.abs(kj - WIN_BLOCK - qi) <= WINDOW) & (key_pos >= 0) & (key_pos < s)
    scale = DH_A ** -0.5
    sink_l = sink.reshape(H_KV, GQA_GROUP)

    def block(args):
        qn, kn, vn, mask = args
        s_ctx = jnp.einsum('bqhgd,bchd->bhgqc', qn, k_ctx) * scale
        s_win = jnp.einsum('bqhgd,bkhd->bhgqk', qn, kn) * scale
        s_win = jnp.where(mask, s_win, -jnp.inf)
        sink_col = jnp.broadcast_to(sink_l[None, :, :, None, None], s_ctx.shape[:-1] + (1,))
        p = jax.nn.softmax(jnp.concatenate([sink_col, s_ctx, s_win], -1), -1)
        return (jnp.einsum('bhgqc,bchd->bqhgd', p[..., 1:1 + lc], v_ctx)
                + jnp.einsum('bhgqk,bkhd->bqhgd', p[..., 1 + lc:], vn))

    out = lax.map(block, (qb, kw, vw, valid))
    return jnp.moveaxis(out, 0, 1).reshape(b, s, H_A, DH_A)


def _context_attention(q, k, v, sink):
    b, lc = q.shape[:2]
    qg = q.reshape(b, lc, H_KV, GQA_GROUP, DH_A)
    sc = jnp.einsum('bqhgd,bkhd->bhgqk', qg, k) * DH_A ** -0.5
    sink_col = jnp.broadcast_to(sink.reshape(H_KV, GQA_GROUP)[None, :, :, None, None], sc.shape[:-1] + (1,))
    p = jax.nn.softmax(jnp.concatenate([sink_col, sc], -1), -1)[..., 1:]
    return jnp.einsum('bhgqk,bkhd->bqhgd', p, v).reshape(b, lc, H_A, DH_A)


def _ab_streams(h, w_in, gate_b):
    q_m, k_m, v_m, o_m, g_m, q_a, k_a, v_a = _split(_proj(h, w_in), AB_WIDTHS)
    b, s = h.shape[:2]
    g = g_m.reshape(b, s, 2, 2, H_M) + gate_b
    mlstm = (_heads(q_m, H_M), _heads(k_m, H_M) * DH_M ** -0.5, _heads(v_m, H_M))
    gates = [(g[:, :, d, 0].transpose(0, 2, 1), jax.nn.log_sigmoid(g[:, :, d, 1]).transpose(0, 2, 1))
             for d in range(2)]
    attn = (q_a.reshape(b, s, H_A, DH_A), k_a.reshape(b, s, H_KV, DH_A), v_a.reshape(b, s, H_KV, DH_A))
    return mlstm, gates, o_m, attn


def _merge_ab(m, o, a, norm_w, w_out):
    hm = _head_norm(m, norm_w) * jax.nn.sigmoid(o)
    cat = jnp.concatenate([hm, a.reshape(a.shape[0], a.shape[1], A_Q)], -1)
    return _proj(cat, w_out)


def _mixer_ab(hl, hc, w_in, gate_b, norm_w, sink, w_out, ctx_out):
    ml, gl, ol, (ql, kl, vl) = _ab_streams(hl, w_in, gate_b)
    mc, gc, oc, (qc, kc, vc) = _ab_streams(hc, w_in, gate_b)
    b = hl.shape[0]
    init = (jnp.zeros((b, H_M, DH_M, DH_M), jnp.float32), jnp.zeros((b, H_M, DH_M), jnp.float32),
            jnp.zeros((b, H_M), jnp.float32))
    outs = [_directional(_mlstm_scan, mc + gc[d], ml + gl[d], init, d == 1) for d in range(2)]
    cos, sin = _axial_rope(hl.shape[1])
    a_lat = _window_attention(_rope(ql, cos, sin), _rope(kl, cos, sin), vl, kc, vc, sink)
    y_lat = _merge_ab(outs[0][1] + outs[1][1], ol, a_lat, norm_w, w_out)
    if not ctx_out:
        return y_lat, None
    y_ctx = _merge_ab(outs[0][0] + outs[1][0], oc, _context_attention(qc, kc, vc, sink), norm_w, w_out)
    return y_lat, y_ctx


def _c_streams(h, w_in, gate_up, gate_b):
    q, k, v, g, low = _split(_proj(h, w_in), C_WIDTHS)
    b, s = h.shape[:2]
    qkv = (_heads(q, H_C) * DK_C ** -0.5, _heads(k, H_C), _heads(v, H_C))
    low = low.reshape(b, s, 2, GATE_RANK)
    log_a = [_heads(jax.nn.log_sigmoid(low[:, :, d] @ gate_up[d] + gate_b[d]) / GATE_TAU, H_C) for d in range(2)]
    return qkv, log_a, g


def _merge_c(h, g, norm_w, w_out):
    return _proj(_head_norm(h, norm_w) * jax.nn.silu(g), w_out)


GLA_SUB = 16


def _gla_body(reverse, q_ref, k_ref, v_ref, cum_ref, o_ref, st_t):
    @pl.when(pl.program_id(1) == 0)
    def _():
        st_t[...] = jnp.zeros_like(st_t)

    bf = jnp.bfloat16
    nt = (((1,), (1,)), ((), ()))
    q, k, v, cum = q_ref[0], k_ref[0], v_ref[0], cum_ref[0]
    vb = v.astype(bf)
    inter = lax.dot_general((q * jnp.exp(cum)).astype(bf), st_t[...].astype(bf), nt,
                            preferred_element_type=jnp.float32)
    row = lax.broadcasted_iota(jnp.int32, (GLA_SUB, DK_C), 0)
    blocks = []
    for i in range(CHUNK // GLA_SUB):
        lo = i * GLA_SUB
        qi, ki, vi, ci = q[lo:lo + GLA_SUB], k[lo:lo + GLA_SUB], v[lo:lo + GLA_SUB], cum[lo:lo + GLA_SUB]
        acc = inter[lo:lo + GLA_SUB]
        if reverse:
            seen, p = slice(lo + GLA_SUB, CHUNK), cum[lo + GLA_SUB - 1:lo + GLA_SUB]
        else:
            seen, p = slice(0, lo), cum[lo:lo + 1]
        if seen.stop > seen.start:
            a = (qi * jnp.exp(ci - p)).astype(bf)
            b = (k[seen] * jnp.exp(p - cum[seen])).astype(bf)
            sc = lax.dot_general(a, b, nt, preferred_element_type=jnp.float32)
            acc = acc + jnp.dot(sc.astype(bf), vb[seen], preferred_element_type=jnp.float32)
        for s in range(GLA_SUB):
            visible = (row <= s) if reverse else (row >= s)
            e = jnp.where(visible, jnp.exp(ci - ci[s:s + 1]), 0.0)
            col = jnp.sum(qi * ki[s:s + 1] * e, axis=1, keepdims=True)
            acc = acc + col * vi[s:s + 1]
        blocks.append(acc)
    o_ref[0] = jnp.concatenate(blocks, axis=0)

    total = cum[0:1] if reverse else cum[CHUNK - 1:CHUNK]
    kd = k * jnp.exp(total - cum)
    pad = jnp.zeros((LANE - CHUNK, DK_C), jnp.float32)
    vpad = jnp.zeros((LANE - CHUNK, DV_C), jnp.float32)
    v_t = jnp.concatenate([v, vpad], axis=0).T
    st_t[...] = st_t[...] * jnp.exp(total) + jnp.dot(v_t.astype(bf), jnp.concatenate([kd, pad], axis=0).astype(bf),
                                                      preferred_element_type=jnp.float32)


def _gla_chunks(q, k, v, cum, reverse):
    n, s = q.shape[:2]
    assert s % CHUNK == 0
    last = s // CHUNK - 1
    chunk_at = (lambda i, c: (i, last - c, 0)) if reverse else (lambda i, c: (i, c, 0))
    kspec = pl.BlockSpec((1, CHUNK, DK_C), chunk_at)
    vspec = pl.BlockSpec((1, CHUNK, DV_C), chunk_at)
    return pl.pallas_call(
        functools.partial(_gla_body, reverse),
        grid=(n, s // CHUNK),
        in_specs=[kspec, kspec, vspec, kspec],
        out_specs=vspec,
        out_shape=jax.ShapeDtypeStruct((n, s, DV_C), jnp.float32),
        scratch_shapes=[pltpu.VMEM((DV_C, DK_C), jnp.float32)],
        compiler_params=pltpu.CompilerParams(dimension_semantics=("arbitrary", "arbitrary")),
        name="gla_chunks",
    )(q, k, v, cum)


def _gla_bidirectional(sc, ac, sl, al):
    lc, ls = sc[0].shape[2], sl[0].shape[2]
    lead = sl[2].shape[:2]

    def seq(ctx_a, lat_a, reverse):
        a = jnp.concatenate([lat_a, ctx_a] if reverse else [ctx_a, lat_a], axis=2)
        return a.reshape((-1,) + a.shape[2:])

    def chunk_cumsum(a, reverse):
        n, s, d = a.shape
        return lax.cumsum(a.reshape(n, s // CHUNK, CHUNK, d), axis=2, reverse=reverse).reshape(n, s, d)

    outs = []
    for d in range(2):
        rev = d == 1
        q, k, v = (seq(c_, l_, rev) for c_, l_ in zip(sc, sl))
        cum = chunk_cumsum(seq(ac[d], al[d], rev), rev)
        outs.append(_gla_chunks(q, k, v, cum, rev).reshape(lead + (-1, DV_C)))
    fwd, bwd = outs
    h_ctx = fwd[:, :, :lc] + bwd[:, :, ls:]
    h_lat = fwd[:, :, lc:] + bwd[:, :, :ls]
    return h_ctx, h_lat


def _mixer_c(hl, hc, w_in, gate_up, gate_b, norm_w, w_out, ctx_out):
    sl, al, gl = _c_streams(hl, w_in, gate_up, gate_b)
    sc, ac, gc = _c_streams(hc, w_in, gate_up, gate_b)
    h_ctx, h_lat = _gla_bidirectional(sc, ac, sl, al)
    y_lat = _merge_c(h_lat, gl, norm_w, w_out)
    if not ctx_out:
        return y_lat, None
    return y_lat, _merge_c(h_ctx, gc, norm_w, w_out)


PEER_GROUP = 8
PEER_RING = 4
PEER_AHEAD = 3
PEER_TB = 256
DMA_QUEUES = 2
PEER_VMEM_BYTES = (PEER_RING * PEER_GROUP * 2 * D_MODEL * PEER_HEADS * PEER_TOPK * 4
                   + 2 * PEER_TB * (2 * D_MODEL + PEER_HEADS * PEER_TOPK) * 4 + (8 << 20))
N_PICK = PEER_HEADS * PEER_TOPK
MXU_ROWS = 16
SQRT_HALF = 0.7071067811865476
UV_ROWS = 2 * D_MODEL // LANE


def _hi_lo_rows(row):
    n = row.shape[1]
    hi = row.astype(jnp.bfloat16).astype(jnp.float32)
    lo = row - hi
    sub = lax.broadcasted_iota(jnp.int32, (MXU_ROWS, n), 0)
    full = jnp.where(sub == 0, jnp.broadcast_to(hi, (MXU_ROWS, n)),
                     jnp.where(sub == 1, jnp.broadcast_to(lo, (MXU_ROWS, n)), 0.0))
    return full.astype(jnp.bfloat16)


def _peer_expert_body(eid_hbm, h_ref, gate_ref, uv_hbm, out_ref, eid_s, *rest):
    bufs, (row_sems, eid_sem) = rest[:PEER_RING], rest[PEER_RING:]
    i = pl.program_id(0)
    nsteps = pl.num_programs(0)
    cur = i % 2
    nxt = 1 - cur
    n_groups = PEER_TB // PEER_GROUP
    ahead_tokens = PEER_AHEAD * PEER_GROUP

    def eid_copies(step, slot):
        blk = jnp.minimum(step, nsteps - 1)
        tail = jnp.minimum(step + 1, nsteps - 1)
        return (pltpu.make_async_copy(eid_hbm.at[blk], eid_s.at[slot, pl.ds(0, PEER_TB)], eid_sem),
                pltpu.make_async_copy(eid_hbm.at[tail, pl.ds(0, ahead_tokens)],
                                      eid_s.at[slot, pl.ds(PEER_TB, ahead_tokens)], eid_sem))

    def issue_token(base, k, which):
        for e in range(N_PICK):
            pltpu.make_async_copy(uv_hbm.at[eid_s[cur, base + k, e]], bufs[which].at[k, :, e, :],
                                  row_sems.at[which]).start(priority=e % DMA_QUEUES)

    def issue(base, which):
        for k in range(PEER_GROUP):
            issue_token(base, k, which)

    def wait_rows(which):
        pltpu.make_async_copy(bufs[which], bufs[which], row_sems.at[which]).wait()

    @pl.when(i == 0)
    def _():
        for c in eid_copies(0, 0):
            c.start()
        for c in eid_copies(0, 0):
            c.wait()
        for g in range(PEER_AHEAD):
            issue(g * PEER_GROUP, g)

    for c in eid_copies(i + 1, nxt):
        c.start()

    n_chunk = D_MODEL // LANE

    def compute(base, which):
        buf = bufs[which]
        base = pl.multiple_of(base, PEER_GROUP)
        h8 = h_ref[pl.ds(base, PEER_GROUP), :]
        g8 = gate_ref[pl.ds(base, PEER_GROUP), :]
        rows = []
        for k in range(PEER_GROUP):
            issue_token(base + ahead_tokens, k, (which + PEER_AHEAD) % PEER_RING)
            gu = jnp.concatenate([buf[k, s] for s in range(n_chunk)], axis=1).astype(jnp.bfloat16)
            z2 = lax.dot_general(_hi_lo_rows(h8[k:k + 1]), gu, (((1,), (1,)), ((), ())),
                                 preferred_element_type=jnp.float32)
            z = z2[0:1] + z2[1:2]
            act = 0.5 * z * (1.0 + lax.erf(z * SQRT_HALF))
            gv = jnp.concatenate([buf[k, n_chunk + s] for s in range(n_chunk)], axis=1).astype(jnp.bfloat16)
            o2 = jnp.dot(_hi_lo_rows(g8[k:k + 1] * act), gv, preferred_element_type=jnp.float32)
            rows.append(o2[0:1] + o2[1:2])
        out_ref[pl.ds(base, PEER_GROUP), :] = jnp.concatenate(rows, axis=0)

    def ring_turn(j, carry):
        for w in range(PEER_RING):
            wait_rows(w)
            compute((j * PEER_RING + w) * PEER_GROUP, w)
        return carry

    lax.fori_loop(0, n_groups // PEER_RING, ring_turn, 0)
    for c in eid_copies(i + 1, nxt):
        c.wait()

    @pl.when(i == nsteps - 1)
    def _():
        for g in range(PEER_AHEAD):
            wait_rows(g)


def _peer_experts(h, eid, gate, uv):
    t = h.shape[0]
    assert t % PEER_TB == 0 and PEER_TB % (PEER_RING * PEER_GROUP) == 0 and 0 < PEER_AHEAD < PEER_RING
    nb = t // PEER_TB
    group_buf = pltpu.VMEM((PEER_GROUP, UV_ROWS, N_PICK, LANE), jnp.float32)
    return pl.pallas_call(
        _peer_expert_body,
        grid=(nb,),
        in_specs=[pl.BlockSpec(memory_space=pl.ANY),
                  pl.BlockSpec((PEER_TB, D_MODEL), lambda i: (i, 0)),
                  pl.BlockSpec((PEER_TB, N_PICK), lambda i: (i, 0)),
                  pl.BlockSpec(memory_space=pl.ANY)],
        out_specs=pl.BlockSpec((PEER_TB, D_MODEL), lambda i: (i, 0)),
        out_shape=jax.ShapeDtypeStruct((t, D_MODEL), jnp.float32),
        scratch_shapes=[pltpu.SMEM((2, PEER_TB + PEER_AHEAD * PEER_GROUP, N_PICK), jnp.int32)]
        + [group_buf] * PEER_RING
        + [pltpu.SemaphoreType.DMA((PEER_RING,)), pltpu.SemaphoreType.DMA],
        compiler_params=pltpu.CompilerParams(dimension_semantics=("arbitrary",),
                                             vmem_limit_bytes=PEER_VMEM_BYTES),
        name="peer_experts",
    )(eid.reshape(nb, PEER_TB, N_PICK), h, gate, uv)


SEL_TB = 256


def _top_rows(s, ids, k):
    n = s.shape[0]
    row = lax.broadcasted_iota(jnp.int32, s.shape, 0).astype(jnp.float32)
    vals, picks = [], []
    for _ in range(k):
        m = jnp.max(s, axis=0, keepdims=True)
        first = jnp.min(jnp.where(s == m, row, float(n)), axis=0, keepdims=True)
        hit = row == first
        vals.append(m)
        picks.append(first if ids is None else jnp.sum(jnp.where(hit, ids, 0.0), axis=0, keepdims=True))
        s = jnp.where(hit, -jnp.inf, s)
    return jnp.concatenate(vals, axis=0), jnp.concatenate(picks, axis=0)


def _peer_select_body(h_ref, wq_ref, keys_ref, eid_ref, gate_ref, q_s, eid_t, gate_t):
    q_s[...] = jnp.dot(h_ref[...].astype(jnp.bfloat16), wq_ref[...], preferred_element_type=jnp.float32)

    def head(hd, carry):
        tops = []
        for p in range(2):
            col = pl.multiple_of((hd * 2 + p) * PEER_HALF, PEER_HALF)
            q_hp = q_s[:, pl.ds(col, PEER_HALF)].astype(jnp.bfloat16)
            s = lax.dot_general(keys_ref[hd * 2 + p], q_hp, (((1,), (1,)), ((), ())),
                                preferred_element_type=jnp.float32)
            tops.append(_top_rows(s, None, PEER_TOPK))
        (v0, i0), (v1, i1) = tops
        cand = jnp.concatenate([v0[i:i + 1] + v1 for i in range(PEER_TOPK)], axis=0)
        cid = jnp.concatenate([i0[i:i + 1] * float(N_KEYS) + i1 for i in range(PEER_TOPK)], axis=0)
        best, eid = _top_rows(cand, cid, PEER_TOPK)
        ex = jnp.exp(best - best[0:1])
        row0 = pl.multiple_of(hd * PEER_TOPK, PEER_TOPK)
        eid_t[pl.ds(row0, PEER_TOPK), :] = eid
        gate_t[pl.ds(row0, PEER_TOPK), :] = ex / jnp.sum(ex, axis=0, keepdims=True)
        return carry

    lax.fori_loop(0, PEER_HEADS, head, 0)
    eid_ref[...] = eid_t[...].T.astype(jnp.int32)
    gate_ref[...] = gate_t[...].T


def _peer_select(h, wq, keys):
    t = h.shape[0]
    assert t % SEL_TB == 0
    qw = PEER_HEADS * PEER_KEY_DIM
    return pl.pallas_call(
        _peer_select_body,
        grid=(t // SEL_TB,),
        in_specs=[pl.BlockSpec((SEL_TB, D_MODEL), lambda i: (i, 0)),
                  pl.BlockSpec((D_MODEL, qw), lambda i: (0, 0)),
                  pl.BlockSpec((PEER_HEADS * 2, N_KEYS, PEER_HALF), lambda i: (0, 0, 0))],
        out_specs=[pl.BlockSpec((SEL_TB, N_PICK), lambda i: (i, 0)),
                   pl.BlockSpec((SEL_TB, N_PICK), lambda i: (i, 0))],
        out_shape=[jax.ShapeDtypeStruct((t, N_PICK), jnp.int32),
                   jax.ShapeDtypeStruct((t, N_PICK), jnp.float32)],
        scratch_shapes=[pltpu.VMEM((SEL_TB, qw), jnp.float32),
                        pltpu.VMEM((N_PICK, SEL_TB), jnp.float32),
                        pltpu.VMEM((N_PICK, SEL_TB), jnp.float32)],
        name="peer_select",
    )(h, wq.astype(jnp.bfloat16), keys.reshape(PEER_HEADS * 2, N_KEYS, PEER_HALF).astype(jnp.bfloat16))


def _peer(h, wq, keys, uv):
    shape = h.shape
    h2 = h.reshape(-1, shape[-1])
    eid, gate = _peer_select(h2, wq, keys)
    return _peer_experts(h2, eid, gate, uv).reshape(shape)


def kernel(x, c, ctx, c_ctx, w_mod, b_mod, ln_w, ln_b, ab_w_in, ab_gate_b, ab_norm_w, ab_sink, ab_w_out,
           gla_w_in, gla_gate_up, gla_gate_b, gla_norm_w, gla_w_out, peer_wq, peer_keys, peer_u, peer_v):
    for layer in range(DEPTH):
        last = layer == DEPTH - 1
        j = layer // 2
        mod_l = jax.nn.silu(c) @ w_mod[layer] + b_mod[layer]
        mod_c = jax.nn.silu(c_ctx) @ w_mod[layer] + b_mod[layer]
        sh1, sc1, g1, sh2, sc2, g2 = jnp.split(mod_l[:, None, :], 6, axis=-1)
        csh1, csc1, cg1, csh2, csc2, cg2 = jnp.split(mod_c, 6)
        hl = x * (1.0 + sc1) + sh1
        hc = ctx * (1.0 + csc1) + csh1
        if layer % 2 == 0:
            y_lat, y_ctx = _mixer_ab(hl, hc, ab_w_in[j], ab_gate_b[j], ab_norm_w[j], ab_sink[j], ab_w_out[j], not last)
        else:
            y_lat, y_ctx = _mixer_c(hl, hc, gla_w_in[j], gla_gate_up[j], gla_gate_b[j], gla_norm_w[j], gla_w_out[j],
                                    not last)
        x = _layernorm(DEEPNORM_ALPHA * x + g1 * y_lat, ln_w[layer, 0], ln_b[layer, 0])
        uv = jnp.concatenate([peer_u[layer].reshape(-1, UV_ROWS // 2, LANE),
                              peer_v[layer].reshape(-1, UV_ROWS // 2, LANE)], axis=1)
        f_lat = _peer(x * (1.0 + sc2) + sh2, peer_wq[layer], peer_keys[layer], uv)
        x = _layernorm(DEEPNORM_ALPHA * x + g2 * f_lat, ln_w[layer, 1], ln_b[layer, 1])
        if not last:
            ctx = _layernorm(DEEPNORM_ALPHA * ctx + cg1 * y_ctx, ln_w[layer, 0], ln_b[layer, 0])
            f_ctx = _peer(ctx * (1.0 + csc2) + csh2, peer_wq[layer], peer_keys[layer], uv)
            ctx = _layernorm(DEEPNORM_ALPHA * ctx + cg2 * f_ctx, ln_w[layer, 1], ln_b[layer, 1])
    return x
```

```python
import functools

import jax
import jax.numpy as jnp
import numpy as np
from jax import lax
from jax.experimental import pallas as pl
from jax.experimental.pallas import tpu as pltpu

D_MODEL = 1024
DEPTH = 2
GRID_W = 64
H_M = 4
DH_M = 128
M_W = H_M * DH_M
H_A = 8
H_KV = 2
DH_A = 64
A_Q = H_A * DH_A
A_KV = H_KV * DH_A
GQA_GROUP = H_A // H_KV
WINDOW = 128
WIN_BLOCK = 128
ROPE_THETA = 10000.0
AB_WIDTHS = (M_W, M_W, M_W, M_W, 4 * H_M, A_Q, A_KV, A_KV)
MIX_W = M_W + A_Q
H_C = 4
DK_C = 128
DV_C = 256
C_K = H_C * DK_C
C_V = H_C * DV_C
GATE_RANK = 16
GATE_TAU = 16.0
C_WIDTHS = (C_K, C_K, C_V, C_V, 2 * GATE_RANK)
CHUNK = 64
N_KEYS = 128
PEER_HEADS = 8
PEER_TOPK = 16
PEER_KEY_DIM = 256
PEER_HALF = PEER_KEY_DIM // 2
PEER_CHUNK = 128
DEEPNORM_ALPHA = (2 * DEPTH) ** 0.25
LN_EPS = 1e-5

LANE = 128
MM_TILE_M = 512
MM_TILE_N = 512


def _matmul_body(x_ref, w_ref, o_ref):
    o_ref[...] = jnp.dot(x_ref[...].astype(jnp.bfloat16), w_ref[...].astype(jnp.bfloat16),
                         preferred_element_type=jnp.float32)


def _matmul(x, w):
    m, k = x.shape
    n = w.shape[1]
    n_pad = -n % MM_TILE_N
    if n_pad:
        w = jnp.pad(w, ((0, 0), (0, n_pad)))
    tm = min(MM_TILE_M, m)
    assert m % tm == 0
    out = pl.pallas_call(
        _matmul_body,
        grid=((n + n_pad) // MM_TILE_N, m // tm),
        in_specs=[pl.BlockSpec((tm, k), lambda j, i: (i, 0)),
                  pl.BlockSpec((k, MM_TILE_N), lambda j, i: (0, j))],
        out_specs=pl.BlockSpec((tm, MM_TILE_N), lambda j, i: (i, j)),
        out_shape=jax.ShapeDtypeStruct((m, n + n_pad), jnp.float32),
        name="proj_matmul",
    )(x, w)
    return out[:, :n] if n_pad else out


def _proj(h, w):
    lead = h.shape[:-1]
    return _matmul(h.reshape(-1, h.shape[-1]), w).reshape(lead + (w.shape[1],))


def _layernorm(x, w, b):
    mu = x.mean(-1, keepdims=True)
    var = jnp.mean(jnp.square(x - mu), -1, keepdims=True)
    return ((x - mu) * lax.rsqrt(var + LN_EPS)) * w + b


def _split(a, widths):
    idx = np.cumsum(widths)[:-1].tolist()
    return jnp.split(a, idx, axis=-1)


def _heads(a, h):
    b, s = a.shape[:2]
    return a.reshape(b, s, h, -1).transpose(0, 2, 1, 3)


def _head_norm(h, w):
    hf = h * lax.rsqrt(jnp.mean(h * h, -1, keepdims=True) + LN_EPS)
    b, nh, s, d = h.shape
    return hf.transpose(0, 2, 1, 3).reshape(b, s, nh * d) * w


def _to_chunks(a):
    b, h, s = a.shape[:3]
    a = a.reshape((b, h, s // CHUNK, CHUNK) + a.shape[3:])
    return jnp.moveaxis(a, 2, 0)


def _from_chunks(a):
    a = jnp.moveaxis(a, 0, 2)
    return a.reshape(a.shape[:2] + (-1,) + a.shape[4:])


def _mlstm_scan(q, k, v, log_i, log_f, state):
    causal = jnp.tril(jnp.ones((CHUNK, CHUNK), bool))

    def step(carry, inp):
        c_st, n_st, m_st = carry
        qc, kc, vc, li, lf = inp
        cum = jnp.cumsum(lf, axis=-1)
        dmat = cum[..., :, None] - cum[..., None, :] + li[..., None, :]
        dmat = jnp.where(causal, dmat, -jnp.inf)
        m_inter = cum + m_st[..., None]
        m_t = jnp.maximum(m_inter, dmat.max(-1))
        w = jnp.exp(dmat - m_t[..., None]) * jnp.einsum('bhtd,bhsd->bhts', qc, kc)
        a_inter = jnp.exp(m_inter - m_t)
        num = a_inter[..., None] * jnp.einsum('bhtd,bhde->bhte', qc, c_st) + jnp.einsum('bhts,bhse->bhte', w, vc)
        den = a_inter * jnp.einsum('bhtd,bhd->bht', qc, n_st) + w.sum(-1)
        h = num / jnp.maximum(jnp.abs(den), jnp.exp(-m_t))[..., None]
        total = cum[..., -1]
        decay_s = total[..., None] - cum + li
        m_new = jnp.maximum(total + m_st, decay_s.max(-1))
        ws = jnp.exp(decay_s - m_new[..., None])
        a_st = jnp.exp(total + m_st - m_new)
        c_st = a_st[..., None, None] * c_st + jnp.einsum('bhs,bhsd,bhse->bhde', ws, kc, vc)
        n_st = a_st[..., None] * n_st + jnp.einsum('bhs,bhsd->bhd', ws, kc)
        return (c_st, n_st, m_new), h

    state, hs = lax.scan(step, state, tuple(_to_chunks(a) for a in (q, k, v, log_i, log_f)))
    return _from_chunks(hs), state


def _gla_scan(q, k, v, log_a, state):
    causal = jnp.tril(jnp.ones((CHUNK, CHUNK), bool))

    def step(st, inp):
        qc, kc, vc, la = inp
        cum = jnp.cumsum(la, axis=2)
        rel = cum[:, :, :, None, :] - cum[:, :, None, :, :]
        rel = jnp.where(causal[:, :, None], rel, -jnp.inf)
        scores = jnp.einsum('bhtk,bhsk,bhtsk->bhts', qc, kc, jnp.exp(rel))
        out = jnp.einsum('bhtk,bhkv->bhtv', qc * jnp.exp(cum), st) + jnp.einsum('bhts,bhsv->bhtv', scores, vc)
        total = cum[:, :, -1:, :]
        st = jnp.exp(total[:, :, 0, :, None]) * st + jnp.einsum('bhsk,bhsv->bhkv', kc * jnp.exp(total - cum), vc)
        return st, out

    state, outs = lax.scan(step, state, tuple(_to_chunks(a) for a in (q, k, v, log_a)))
    return _from_chunks(outs), state


def _directional(scan_fn, ctx_seq, lat_seq, init, reverse):
    if reverse:
        ctx_seq = tuple(jnp.flip(a, 2) for a in ctx_seq)
        lat_seq = tuple(jnp.flip(a, 2) for a in lat_seq)
    h_ctx, state = scan_fn(*ctx_seq, init)
    h_lat, _ = scan_fn(*lat_seq, state)
    if reverse:
        h_ctx, h_lat = jnp.flip(h_ctx, 2), jnp.flip(h_lat, 2)
    return h_ctx, h_lat


def _axial_rope(s):
    rows = s // GRID_W
    row = jnp.repeat(jnp.arange(rows), GRID_W).astype(jnp.float32)
    col = jnp.tile(jnp.arange(GRID_W), rows).astype(jnp.float32)
    n_freq = DH_A // 4
    inv = ROPE_THETA ** (-jnp.arange(n_freq, dtype=jnp.float32) / n_freq)
    ang = jnp.concatenate([row[:, None] * inv, col[:, None] * inv], -1)
    return jnp.cos(ang), jnp.sin(ang)


def _rope(x, cos, sin):
    x1, x2 = x[..., 0::2], x[..., 1::2]
    c = cos[None, :, None, :]
    sn = sin[None, :, None, :]
    return jnp.stack([x1 * c - x2 * sn, x1 * sn + x2 * c], -1).reshape(x.shape)


def _window_attention(q, k, v, k_ctx, v_ctx, sink):
    b, s = q.shape[:2]
    nb = s // WIN_BLOCK
    lc = k_ctx.shape[1]
    qb = jnp.moveaxis(q.reshape(b, nb, WIN_BLOCK, H_KV, GQA_GROUP, DH_A), 1, 0)

    def bands(a):
        ap = jnp.pad(a, ((0, 0), (WIN_BLOCK, WIN_BLOCK), (0, 0), (0, 0))).reshape(b, nb + 2, WIN_BLOCK, H_KV, DH_A)
        return jnp.moveaxis(jnp.concatenate([ap[:, :-2], ap[:, 1:-1], ap[:, 2:]], 2), 1, 0)

    kw, vw = bands(k), bands(v)
    qi = jnp.arange(WIN_BLOCK)[:, None]
    kj = jnp.arange(3 * WIN_BLOCK)[None, :]
    key_pos = jnp.arange(nb)[:, None, None] * WIN_BLOCK + kj - WIN_BLOCK
    valid = (jnp.abs(kj - WIN_BLOCK - qi) <= WINDOW) & (key_pos >= 0) & (key_pos < s)
    scale = DH_A ** -0.5
    sink_l = sink.reshape(H_KV, GQA_GROUP)

    def block(args):
        qn, kn, vn, mask = args
        s_ctx = jnp.einsum('bqhgd,bchd->bhgqc', qn, k_ctx) * scale
        s_win = jnp.einsum('bqhgd,bkhd->bhgqk', qn, kn) * scale
        s_win = jnp.where(mask, s_win, -jnp.inf)
        sink_col = jnp.broadcast_to(sink_l[None, :, :, None, None], s_ctx.shape[:-1] + (1,))
        p = jax.nn.softmax(jnp.concatenate([sink_col, s_ctx, s_win], -1), -1)
        return (jnp.einsum('bhgqc,bchd->bqhgd', p[..., 1:1 + lc], v_ctx)
                + jnp.einsum('bhgqk,bkhd->bqhgd', p[..., 1 + lc:], vn))

    out = lax.map(block, (qb, kw, vw, valid))
    return jnp.moveaxis(out, 0, 1).reshape(b, s, H_A, DH_A)


def _context_attention(q, k, v, sink):
    b, lc = q.shape[:2]
    qg = q.reshape(b, lc, H_KV, GQA_GROUP, DH_A)
    sc = jnp.einsum('bqhgd,bkhd->bhgqk', qg, k) * DH_A ** -0.5
    sink_col = jnp.broadcast_to(sink.reshape(H_KV, GQA_GROUP)[None, :, :, None, None], sc.shape[:-1] + (1,))
    p = jax.nn.softmax(jnp.concatenate([sink_col, sc], -1), -1)[..., 1:]
    return jnp.einsum('bhgqk,bkhd->bqhgd', p, v).reshape(b, lc, H_A, DH_A)


def _ab_streams(h, w_in, gate_b):
    q_m, k_m, v_m, o_m, g_m, q_a, k_a, v_a = _split(_proj(h, w_in), AB_WIDTHS)
    b, s = h.shape[:2]
    g = g_m.reshape(b, s, 2, 2, H_M) + gate_b
    mlstm = (_heads(q_m, H_M), _heads(k_m, H_M) * DH_M ** -0.5, _heads(v_m, H_M))
    gates = [(g[:, :, d, 0].transpose(0, 2, 1), jax.nn.log_sigmoid(g[:, :, d, 1]).transpose(0, 2, 1))
             for d in range(2)]
    attn = (q_a.reshape(b, s, H_A, DH_A), k_a.reshape(b, s, H_KV, DH_A), v_a.reshape(b, s, H_KV, DH_A))
    return mlstm, gates, o_m, attn


def _merge_ab(m, o, a, norm_w, w_out):
    hm = _head_norm(m, norm_w) * jax.nn.sigmoid(o)
    cat = jnp.concatenate([hm, a.reshape(a.shape[0], a.shape[1], A_Q)], -1)
    return _proj(cat, w_out)


def _mixer_ab(hl, hc, w_in, gate_b, norm_w, sink, w_out, ctx_out):
    ml, gl, ol, (ql, kl, vl) = _ab_streams(hl, w_in, gate_b)
    mc, gc, oc, (qc, kc, vc) = _ab_streams(hc, w_in, gate_b)
    b = hl.shape[0]
    init = (jnp.zeros((b, H_M, DH_M, DH_M), jnp.float32), jnp.zeros((b, H_M, DH_M), jnp.float32),
            jnp.zeros((b, H_M), jnp.float32))
    outs = [_directional(_mlstm_scan, mc + gc[d], ml + gl[d], init, d == 1) for d in range(2)]
    cos, sin = _axial_rope(hl.shape[1])
    a_lat = _window_attention(_rope(ql, cos, sin), _rope(kl, cos, sin), vl, kc, vc, sink)
    y_lat = _merge_ab(outs[0][1] + outs[1][1], ol, a_lat, norm_w, w_out)
    if not ctx_out:
        return y_lat, None
    y_ctx = _merge_ab(outs[0][0] + outs[1][0], oc, _context_attention(qc, kc, vc, sink), norm_w, w_out)
    return y_lat, y_ctx


def _c_streams(h, w_in, gate_up, gate_b):
    q, k, v, g, low = _split(_proj(h, w_in), C_WIDTHS)
    b, s = h.shape[:2]
    qkv = (_heads(q, H_C) * DK_C ** -0.5, _heads(k, H_C), _heads(v, H_C))
    low = low.reshape(b, s, 2, GATE_RANK)
    log_a = [_heads(jax.nn.log_sigmoid(low[:, :, d] @ gate_up[d] + gate_b[d]) / GATE_TAU, H_C) for d in range(2)]
    return qkv, log_a, g


def _merge_c(h, g, norm_w, w_out):
    return _proj(_head_norm(h, norm_w) * jax.nn.silu(g), w_out)


GLA_SUB = 16


def _gla_body(reverse, q_ref, k_ref, v_ref, cum_ref, o_ref, st_t):
    @pl.when(pl.program_id(1) == 0)
    def _():
        st_t[...] = jnp.zeros_like(st_t)

    bf = jnp.bfloat16
    nt = (((1,), (1,)), ((), ()))
    q, k, v, cum = q_ref[0], k_ref[0], v_ref[0], cum_ref[0]
    vb = v.astype(bf)
    inter = lax.dot_general((q * jnp.exp(cum)).astype(bf), st_t[...].astype(bf), nt,
                            preferred_element_type=jnp.float32)
    row = lax.broadcasted_iota(jnp.int32, (GLA_SUB, DK_C), 0)
    blocks = []
    for i in range(CHUNK // GLA_SUB):
        lo = i * GLA_SUB
        qi, ki, vi, ci = q[lo:lo + GLA_SUB], k[lo:lo + GLA_SUB], v[lo:lo + GLA_SUB], cum[lo:lo + GLA_SUB]
        acc = inter[lo:lo + GLA_SUB]
        if reverse:
            seen, p = slice(lo + GLA_SUB, CHUNK), cum[lo + GLA_SUB - 1:lo + GLA_SUB]
        else:
            seen, p = slice(0, lo), cum[lo:lo + 1]
        if seen.stop > seen.start:
            a = (qi * jnp.exp(ci - p)).astype(bf)
            b = (k[seen] * jnp.exp(p - cum[seen])).astype(bf)
            sc = lax.dot_general(a, b, nt, preferred_element_type=jnp.float32)
            acc = acc + jnp.dot(sc.astype(bf), vb[seen], preferred_element_type=jnp.float32)
        for s in range(GLA_SUB):
            visible = (row <= s) if reverse else (row >= s)
            e = jnp.where(visible, jnp.exp(ci - ci[s:s + 1]), 0.0)
            col = jnp.sum(qi * ki[s:s + 1] * e, axis=1, keepdims=True)
            acc = acc + col * vi[s:s + 1]
        blocks.append(acc)
    o_ref[0] = jnp.concatenate(blocks, axis=0)

    total = cum[0:1] if reverse else cum[CHUNK - 1:CHUNK]
    kd = k * jnp.exp(total - cum)
    pad = jnp.zeros((LANE - CHUNK, DK_C), jnp.float32)
    vpad = jnp.zeros((LANE - CHUNK, DV_C), jnp.float32)
    v_t = jnp.concatenate([v, vpad], axis=0).T
    st_t[...] = st_t[...] * jnp.exp(total) + jnp.dot(v_t.astype(bf), jnp.concatenate([kd, pad], axis=0).astype(bf),
                                                      preferred_element_type=jnp.float32)


def _gla_chunks(q, k, v, cum, reverse):
    n, s = q.shape[:2]
    assert s % CHUNK == 0
    last = s // CHUNK - 1
    chunk_at = (lambda i, c: (i, last - c, 0)) if reverse else (lambda i, c: (i, c, 0))
    kspec = pl.BlockSpec((1, CHUNK, DK_C), chunk_at)
    vspec = pl.BlockSpec((1, CHUNK, DV_C), chunk_at)
    return pl.pallas_call(
        functools.partial(_gla_body, reverse),
        grid=(n, s // CHUNK),
        in_specs=[kspec, kspec, vspec, kspec],
        out_specs=vspec,
        out_shape=jax.ShapeDtypeStruct((n, s, DV_C), jnp.float32),
        scratch_shapes=[pltpu.VMEM((DV_C, DK_C), jnp.float32)],
        compiler_params=pltpu.CompilerParams(dimension_semantics=("arbitrary", "arbitrary")),
        name="gla_chunks",
    )(q, k, v, cum)


def _gla_bidirectional(sc, ac, sl, al):
    lc, ls = sc[0].shape[2], sl[0].shape[2]
    lead = sl[2].shape[:2]

    def seq(ctx_a, lat_a, reverse):
        a = jnp.concatenate([lat_a, ctx_a] if reverse else [ctx_a, lat_a], axis=2)
        return a.reshape((-1,) + a.shape[2:])

    def chunk_cumsum(a, reverse):
        n, s, d = a.shape
        return lax.cumsum(a.reshape(n, s // CHUNK, CHUNK, d), axis=2, reverse=reverse).reshape(n, s, d)

    outs = []
    for d in range(2):
        rev = d == 1
        q, k, v = (seq(c_, l_, rev) for c_, l_ in zip(sc, sl))
        cum = chunk_cumsum(seq(ac[d], al[d], rev), rev)
        outs.append(_gla_chunks(q, k, v, cum, rev).reshape(lead + (-1, DV_C)))
    fwd, bwd = outs
    h_ctx = fwd[:, :, :lc] + bwd[:, :, ls:]
    h_lat = fwd[:, :, lc:] + bwd[:, :, :ls]
    return h_ctx, h_lat


def _mixer_c(hl, hc, w_in, gate_up, gate_b, norm_w, w_out, ctx_out):
    sl, al, gl = _c_streams(hl, w_in, gate_up, gate_b)
    sc, ac, gc = _c_streams(hc, w_in, gate_up, gate_b)
    h_ctx, h_lat = _gla_bidirectional(sc, ac, sl, al)
    y_lat = _merge_c(h_lat, gl, norm_w, w_out)
    if not ctx_out:
        return y_lat, None
    return y_lat, _merge_c(h_ctx, gc, norm_w, w_out)


PEER_GROUP = 8
PEER_RING = 4
PEER_AHEAD = 3
PEER_TB = 256
DMA_QUEUES = 2
PEER_VMEM_BYTES = (PEER_RING * PEER_GROUP * 2 * D_MODEL * PEER_HEADS * PEER_TOPK * 4
                   + 2 * PEER_TB * (2 * D_MODEL + PEER_HEADS * PEER_TOPK) * 4 + (8 << 20))
N_PICK = PEER_HEADS * PEER_TOPK
MXU_ROWS = 16
SQRT_HALF = 0.7071067811865476
UV_ROWS = 2 * D_MODEL // LANE


def _hi_lo_rows(row):
    n = row.shape[1]
    hi = row.astype(jnp.bfloat16).astype(jnp.float32)
    lo = row - hi
    sub = lax.broadcasted_iota(jnp.int32, (MXU_ROWS, n), 0)
    full = jnp.where(sub == 0, jnp.broadcast_to(hi, (MXU_ROWS, n)),
                     jnp.where(sub == 1, jnp.broadcast_to(lo, (MXU_ROWS, n)), 0.0))
    return full.astype(jnp.bfloat16)


def _peer_expert_body(eid_hbm, h_ref, gate_ref, uv_hbm, out_ref, eid_s, *rest):
    bufs, (row_sems, eid_sem) = rest[:PEER_RING], rest[PEER_RING:]
    i = pl.program_id(0)
    nsteps = pl.num_programs(0)
    cur = i % 2
    nxt = 1 - cur
    n_groups = PEER_TB // PEER_GROUP
    ahead_tokens = PEER_AHEAD * PEER_GROUP

    def eid_copies(step, slot):
        blk = jnp.minimum(step, nsteps - 1)
        tail = jnp.minimum(step + 1, nsteps - 1)
        return (pltpu.make_async_copy(eid_hbm.at[blk], eid_s.at[slot, pl.ds(0, PEER_TB)], eid_sem),
                pltpu.make_async_copy(eid_hbm.at[tail, pl.ds(0, ahead_tokens)],
                                      eid_s.at[slot, pl.ds(PEER_TB, ahead_tokens)], eid_sem))

    def issue_token(base, k, which):
        for e in range(N_PICK):
            pltpu.make_async_copy(uv_hbm.at[eid_s[cur, base + k, e]], bufs[which].at[k, :, e, :],
                                  row_sems.at[which]).start(priority=e % DMA_QUEUES)

    def issue(base, which):
        for k in range(PEER_GROUP):
            issue_token(base, k, which)

    def wait_rows(which):
        pltpu.make_async_copy(bufs[which], bufs[which], row_sems.at[which]).wait()

    @pl.when(i == 0)
    def _():
        for c in eid_copies(0, 0):
            c.start()
        for c in eid_copies(0, 0):
            c.wait()
        for g in range(PEER_AHEAD):
            issue(g * PEER_GROUP, g)

    for c in eid_copies(i + 1, nxt):
        c.start()

    n_chunk = D_MODEL // LANE

    def compute(base, which):
        buf = bufs[which]
        base = pl.multiple_of(base, PEER_GROUP)
        h8 = h_ref[pl.ds(base, PEER_GROUP), :]
        g8 = gate_ref[pl.ds(base, PEER_GROUP), :]
        rows = []
        for k in range(PEER_GROUP):
            issue_token(base + ahead_tokens, k, (which + PEER_AHEAD) % PEER_RING)
            gu = jnp.concatenate([buf[k, s] for s in range(n_chunk)], axis=1).astype(jnp.bfloat16)
            z2 = lax.dot_general(_hi_lo_rows(h8[k:k + 1]), gu, (((1,), (1,)), ((), ())),
                                 preferred_element_type=jnp.float32)
            z = z2[0:1] + z2[1:2]
            act = 0.5 * z * (1.0 + lax.erf(z * SQRT_HALF))
            gv = jnp.concatenate([buf[k, n_chunk + s] for s in range(n_chunk)], axis=1).astype(jnp.bfloat16)
            o2 = jnp.dot(_hi_lo_rows(g8[k:k + 1] * act), gv, preferred_element_type=jnp.float32)
            rows.append(o2[0:1] + o2[1:2])
        out_ref[pl.ds(base, PEER_GROUP), :] = jnp.concatenate(rows, axis=0)

    def ring_turn(j, carry):
        for w in range(PEER_RING):
            wait_rows(w)
            compute((j * PEER_RING + w) * PEER_GROUP, w)
        return carry

    lax.fori_loop(0, n_groups // PEER_RING, ring_turn, 0)
    for c in eid_copies(i + 1, nxt):
        c.wait()

    @pl.when(i == nsteps - 1)
    def _():
        for g in range(PEER_AHEAD):
            wait_rows(g)


def _peer_experts(h, eid, gate, uv):
    t = h.shape[0]
    assert t % PEER_TB == 0 and PEER_TB % (PEER_RING * PEER_GROUP) == 0 and 0 < PEER_AHEAD < PEER_RING
    nb = t // PEER_TB
    group_buf = pltpu.VMEM((PEER_GROUP, UV_ROWS, N_PICK, LANE), jnp.float32)
    return pl.pallas_call(
        _peer_expert_body,
        grid=(nb,),
        in_specs=[pl.BlockSpec(memory_space=pl.ANY),
                  pl.BlockSpec((PEER_TB, D_MODEL), lambda i: (i, 0)),
                  pl.BlockSpec((PEER_TB, N_PICK), lambda i: (i, 0)),
                  pl.BlockSpec(memory_space=pl.ANY)],
        out_specs=pl.BlockSpec((PEER_TB, D_MODEL), lambda i: (i, 0)),
        out_shape=jax.ShapeDtypeStruct((t, D_MODEL), jnp.float32),
        scratch_shapes=[pltpu.SMEM((2, PEER_TB + PEER_AHEAD * PEER_GROUP, N_PICK), jnp.int32)]
        + [group_buf] * PEER_RING
        + [pltpu.SemaphoreType.DMA((PEER_RING,)), pltpu.SemaphoreType.DMA],
        compiler_params=pltpu.CompilerParams(dimension_semantics=("arbitrary",),
                                             vmem_limit_bytes=PEER_VMEM_BYTES),
        name="peer_experts",
    )(eid.reshape(nb, PEER_TB, N_PICK), h, gate, uv)


SEL_TB = 256
SEL_NARROW = 4
RANK_NONE = 1e9


def _top_rows(s, ids, k, row=None):
    if row is None:
        row = lax.broadcasted_iota(jnp.int32, s.shape, 0).astype(jnp.float32)
    vals, picks = [], []
    for _ in range(k):
        m = jnp.max(s, axis=0, keepdims=True)
        first = jnp.min(jnp.where(s == m, row, RANK_NONE), axis=0, keepdims=True)
        hit = row == first
        vals.append(m)
        picks.append(first if ids is None else jnp.sum(jnp.where(hit, ids, 0.0), axis=0, keepdims=True))
        s = jnp.where(hit, -jnp.inf, s)
    return jnp.concatenate(vals, axis=0), jnp.concatenate(picks, axis=0)


def _peer_select_body(h_ref, wq_ref, keys_ref, eid_ref, gate_ref, q_s, eid_t, gate_t):
    q_s[...] = jnp.dot(h_ref[...].astype(jnp.bfloat16), wq_ref[...], preferred_element_type=jnp.float32)

    def head(hd, carry):
        tops = []
        for p in range(2):
            col = pl.multiple_of((hd * 2 + p) * PEER_HALF, PEER_HALF)
            q_hp = q_s[:, pl.ds(col, PEER_HALF)].astype(jnp.bfloat16)
            s = lax.dot_general(keys_ref[hd * 2 + p], q_hp, (((1,), (1,)), ((), ())),
                                preferred_element_type=jnp.float32)
            tops.append(_top_rows(s, None, PEER_TOPK))
        (v0, i0), (v1, i1) = tops
        kf = float(PEER_TOPK)
        r16 = lax.broadcasted_iota(jnp.int32, v0.shape, 0).astype(jnp.float32)
        v0_rest = jnp.where(r16 >= float(SEL_NARROW), v0, -jnp.inf)
        cand = jnp.concatenate([v0[i:i + 1] + v1 for i in range(SEL_NARROW)]
                               + [v0_rest + v1[j:j + 1] for j in range(SEL_NARROW)], axis=0)
        rank = jnp.concatenate([i * kf + r16 for i in range(SEL_NARROW)]
                               + [jnp.where(r16 >= float(SEL_NARROW), r16 * kf + j, RANK_NONE)
                                  for j in range(SEL_NARROW)], axis=0)
        cid = jnp.concatenate([i0[i:i + 1] * float(N_KEYS) + i1 for i in range(SEL_NARROW)]
                              + [i0 * float(N_KEYS) + i1[j:j + 1] for j in range(SEL_NARROW)], axis=0)
        best, eid = _top_rows(cand, cid, PEER_TOPK, rank)
        ex = jnp.exp(best - best[0:1])
        row0 = pl.multiple_of(hd * PEER_TOPK, PEER_TOPK)
        eid_t[pl.ds(row0, PEER_TOPK), :] = eid
        gate_t[pl.ds(row0, PEER_TOPK), :] = ex / jnp.sum(ex, axis=0, keepdims=True)
        return carry

    lax.fori_loop(0, PEER_HEADS, head, 0)
    eid_ref[...] = eid_t[...].T.astype(jnp.int32)
    gate_ref[...] = gate_t[...].T


def _peer_select(h, wq, keys):
    t = h.shape[0]
    assert t % SEL_TB == 0
    qw = PEER_HEADS * PEER_KEY_DIM
    return pl.pallas_call(
        _peer_select_body,
        grid=(t // SEL_TB,),
        in_specs=[pl.BlockSpec((SEL_TB, D_MODEL), lambda i: (i, 0)),
                  pl.BlockSpec((D_MODEL, qw), lambda i: (0, 0)),
                  pl.BlockSpec((PEER_HEADS * 2, N_KEYS, PEER_HALF), lambda i: (0, 0, 0))],
        out_specs=[pl.BlockSpec((SEL_TB, N_PICK), lambda i: (i, 0)),
                   pl.BlockSpec((SEL_TB, N_PICK), lambda i: (i, 0))],
        out_shape=[jax.ShapeDtypeStruct((t, N_PICK), jnp.int32),
                   jax.ShapeDtypeStruct((t, N_PICK), jnp.float32)],
        scratch_shapes=[pltpu.VMEM((SEL_TB, qw), jnp.float32),
                        pltpu.VMEM((N_PICK, SEL_TB), jnp.float32),
                        pltpu.VMEM((N_PICK, SEL_TB), jnp.float32)],
        name="peer_select",
    )(h, wq.astype(jnp.bfloat16), keys.reshape(PEER_HEADS * 2, N_KEYS, PEER_HALF).astype(jnp.bfloat16))


def _peer(h, wq, keys, uv):
    shape = h.shape
    h2 = h.reshape(-1, shape[-1])
    eid, gate = _peer_select(h2, wq, keys)
    return _peer_experts(h2, eid, gate, uv).reshape(shape)


def kernel(x, c, ctx, c_ctx, w_mod, b_mod, ln_w, ln_b, ab_w_in, ab_gate_b, ab_norm_w, ab_sink, ab_w_out,
           gla_w_in, gla_gate_up, gla_gate_b, gla_norm_w, gla_w_out, peer_wq, peer_keys, peer_u, peer_v):
    for layer in range(DEPTH):
        last = layer == DEPTH - 1
        j = layer // 2
        mod_l = jax.nn.silu(c) @ w_mod[layer] + b_mod[layer]
        mod_c = jax.nn.silu(c_ctx) @ w_mod[layer] + b_mod[layer]
        sh1, sc1, g1, sh2, sc2, g2 = jnp.split(mod_l[:, None, :], 6, axis=-1)
        csh1, csc1, cg1, csh2, csc2, cg2 = jnp.split(mod_c, 6)
        hl = x * (1.0 + sc1) + sh1
        hc = ctx * (1.0 + csc1) + csh1
        if layer % 2 == 0:
            y_lat, y_ctx = _mixer_ab(hl, hc, ab_w_in[j], ab_gate_b[j], ab_norm_w[j], ab_sink[j], ab_w_out[j], not last)
        else:
            y_lat, y_ctx = _mixer_c(hl, hc, gla_w_in[j], gla_gate_up[j], gla_gate_b[j], gla_norm_w[j], gla_w_out[j],
                                    not last)
        x = _layernorm(DEEPNORM_ALPHA * x + g1 * y_lat, ln_w[layer, 0], ln_b[layer, 0])
        uv = jnp.concatenate([peer_u[layer].reshape(-1, UV_ROWS // 2, LANE),
                              peer_v[layer].reshape(-1, UV_ROWS // 2, LANE)], axis=1)
        f_lat = _peer(x * (1.0 + sc2) + sh2, peer_wq[layer], peer_keys[layer], uv)
        x = _layernorm(DEEPNORM_ALPHA * x + g2 * f_lat, ln_w[layer, 1], ln_b[layer, 1])
        if not last:
            ctx = _layernorm(DEEPNORM_ALPHA * ctx + cg1 * y_ctx, ln_w[layer, 0], ln_b[layer, 0])
            f_ctx = _peer(ctx * (1.0 + csc2) + csh2, peer_wq[layer], peer_keys[layer], uv)
            ctx = _layernorm(DEEPNORM_ALPHA * ctx + cg2 * f_ctx, ln_w[layer, 1], ln_b[layer, 1])
    return x
```

```python
import functools

import jax
import jax.numpy as jnp
import numpy as np
from jax import lax
from jax.experimental import pallas as pl
from jax.experimental.pallas import tpu as pltpu

D_MODEL = 1024
DEPTH = 2
GRID_W = 64
H_M = 4
DH_M = 128
M_W = H_M * DH_M
H_A = 8
H_KV = 2
DH_A = 64
A_Q = H_A * DH_A
A_KV = H_KV * DH_A
GQA_GROUP = H_A // H_KV
WINDOW = 128
WIN_BLOCK = 128
ROPE_THETA = 10000.0
AB_WIDTHS = (M_W, M_W, M_W, M_W, 4 * H_M, A_Q, A_KV, A_KV)
MIX_W = M_W + A_Q
H_C = 4
DK_C = 128
DV_C = 256
C_K = H_C * DK_C
C_V = H_C * DV_C
GATE_RANK = 16
GATE_TAU = 16.0
C_WIDTHS = (C_K, C_K, C_V, C_V, 2 * GATE_RANK)
CHUNK = 64
N_KEYS = 128
PEER_HEADS = 8
PEER_TOPK = 16
PEER_KEY_DIM = 256
PEER_HALF = PEER_KEY_DIM // 2
PEER_CHUNK = 128
DEEPNORM_ALPHA = (2 * DEPTH) ** 0.25
LN_EPS = 1e-5

LANE = 128
MM_TILE_M = 1024
MM_TILE_N = 512


def _matmul_body(x_ref, w_ref, o_ref):
    o_ref[...] = jnp.dot(x_ref[...].astype(jnp.bfloat16), w_ref[...], preferred_element_type=jnp.float32)


def _matmul(x, w):
    m, k = x.shape
    n = w.shape[1]
    n_pad = -n % MM_TILE_N
    w = w.astype(jnp.bfloat16)
    if n_pad:
        w = jnp.pad(w, ((0, 0), (0, n_pad)))
    tm = min(MM_TILE_M, m)
    assert m % tm == 0
    out = pl.pallas_call(
        _matmul_body,
        grid=((n + n_pad) // MM_TILE_N, m // tm),
        in_specs=[pl.BlockSpec((tm, k), lambda j, i: (i, 0)),
                  pl.BlockSpec((k, MM_TILE_N), lambda j, i: (0, j))],
        out_specs=pl.BlockSpec((tm, MM_TILE_N), lambda j, i: (i, j)),
        out_shape=jax.ShapeDtypeStruct((m, n + n_pad), jnp.float32),
        name="proj_matmul",
    )(x, w)
    return out[:, :n] if n_pad else out


def _proj(h, w):
    lead = h.shape[:-1]
    return _matmul(h.reshape(-1, h.shape[-1]), w).reshape(lead + (w.shape[1],))


def _layernorm(x, w, b):
    mu = x.mean(-1, keepdims=True)
    var = jnp.mean(jnp.square(x - mu), -1, keepdims=True)
    return ((x - mu) * lax.rsqrt(var + LN_EPS)) * w + b


def _split(a, widths):
    idx = np.cumsum(widths)[:-1].tolist()
    return jnp.split(a, idx, axis=-1)


def _heads(a, h):
    b, s = a.shape[:2]
    return a.reshape(b, s, h, -1).transpose(0, 2, 1, 3)


def _head_norm(h, w):
    hf = h * lax.rsqrt(jnp.mean(h * h, -1, keepdims=True) + LN_EPS)
    b, nh, s, d = h.shape
    return hf.transpose(0, 2, 1, 3).reshape(b, s, nh * d) * w


def _to_chunks(a):
    b, h, s = a.shape[:3]
    a = a.reshape((b, h, s // CHUNK, CHUNK) + a.shape[3:])
    return jnp.moveaxis(a, 2, 0)


def _from_chunks(a):
    a = jnp.moveaxis(a, 0, 2)
    return a.reshape(a.shape[:2] + (-1,) + a.shape[4:])


def _mlstm_scan(q, k, v, log_i, log_f, state):
    causal = jnp.tril(jnp.ones((CHUNK, CHUNK), bool))

    def step(carry, inp):
        c_st, n_st, m_st = carry
        qc, kc, vc, li, lf = inp
        cum = jnp.cumsum(lf, axis=-1)
        dmat = cum[..., :, None] - cum[..., None, :] + li[..., None, :]
        dmat = jnp.where(causal, dmat, -jnp.inf)
        m_inter = cum + m_st[..., None]
        m_t = jnp.maximum(m_inter, dmat.max(-1))
        w = jnp.exp(dmat - m_t[..., None]) * jnp.einsum('bhtd,bhsd->bhts', qc, kc)
        a_inter = jnp.exp(m_inter - m_t)
        num = a_inter[..., None] * jnp.einsum('bhtd,bhde->bhte', qc, c_st) + jnp.einsum('bhts,bhse->bhte', w, vc)
        den = a_inter * jnp.einsum('bhtd,bhd->bht', qc, n_st) + w.sum(-1)
        h = num / jnp.maximum(jnp.abs(den), jnp.exp(-m_t))[..., None]
        total = cum[..., -1]
        decay_s = total[..., None] - cum + li
        m_new = jnp.maximum(total + m_st, decay_s.max(-1))
        ws = jnp.exp(decay_s - m_new[..., None])
        a_st = jnp.exp(total + m_st - m_new)
        c_st = a_st[..., None, None] * c_st + jnp.einsum('bhs,bhsd,bhse->bhde', ws, kc, vc)
        n_st = a_st[..., None] * n_st + jnp.einsum('bhs,bhsd->bhd', ws, kc)
        return (c_st, n_st, m_new), h

    state, hs = lax.scan(step, state, tuple(_to_chunks(a) for a in (q, k, v, log_i, log_f)))
    return _from_chunks(hs), state


def _gla_scan(q, k, v, log_a, state):
    causal = jnp.tril(jnp.ones((CHUNK, CHUNK), bool))

    def step(st, inp):
        qc, kc, vc, la = inp
        cum = jnp.cumsum(la, axis=2)
        rel = cum[:, :, :, None, :] - cum[:, :, None, :, :]
        rel = jnp.where(causal[:, :, None], rel, -jnp.inf)
        scores = jnp.einsum('bhtk,bhsk,bhtsk->bhts', qc, kc, jnp.exp(rel))
        out = jnp.einsum('bhtk,bhkv->bhtv', qc * jnp.exp(cum), st) + jnp.einsum('bhts,bhsv->bhtv', scores, vc)
        total = cum[:, :, -1:, :]
        st = jnp.exp(total[:, :, 0, :, None]) * st + jnp.einsum('bhsk,bhsv->bhkv', kc * jnp.exp(total - cum), vc)
        return st, out

    state, outs = lax.scan(step, state, tuple(_to_chunks(a) for a in (q, k, v, log_a)))
    return _from_chunks(outs), state


def _directional(scan_fn, ctx_seq, lat_seq, init, reverse):
    if reverse:
        ctx_seq = tuple(jnp.flip(a, 2) for a in ctx_seq)
        lat_seq = tuple(jnp.flip(a, 2) for a in lat_seq)
    h_ctx, state = scan_fn(*ctx_seq, init)
    h_lat, _ = scan_fn(*lat_seq, state)
    if reverse:
        h_ctx, h_lat = jnp.flip(h_ctx, 2), jnp.flip(h_lat, 2)
    return h_ctx, h_lat


def _axial_rope(s):
    rows = s // GRID_W
    row = jnp.repeat(jnp.arange(rows), GRID_W).astype(jnp.float32)
    col = jnp.tile(jnp.arange(GRID_W), rows).astype(jnp.float32)
    n_freq = DH_A // 4
    inv = ROPE_THETA ** (-jnp.arange(n_freq, dtype=jnp.float32) / n_freq)
    ang = jnp.concatenate([row[:, None] * inv, col[:, None] * inv], -1)
    return jnp.cos(ang), jnp.sin(ang)


def _rope(x, cos, sin):
    x1, x2 = x[..., 0::2], x[..., 1::2]
    c = cos[None, :, None, :]
    sn = sin[None, :, None, :]
    return jnp.stack([x1 * c - x2 * sn, x1 * sn + x2 * c], -1).reshape(x.shape)


def _window_attention(q, k, v, k_ctx, v_ctx, sink):
    b, s = q.shape[:2]
    nb = s // WIN_BLOCK
    lc = k_ctx.shape[1]
    qb = jnp.moveaxis(q.reshape(b, nb, WIN_BLOCK, H_KV, GQA_GROUP, DH_A), 1, 0)

    def bands(a):
        ap = jnp.pad(a, ((0, 0), (WIN_BLOCK, WIN_BLOCK), (0, 0), (0, 0))).reshape(b, nb + 2, WIN_BLOCK, H_KV, DH_A)
        return jnp.moveaxis(jnp.concatenate([ap[:, :-2], ap[:, 1:-1], ap[:, 2:]], 2), 1, 0)

    kw, vw = bands(k), bands(v)
    qi = jnp.arange(WIN_BLOCK)[:, None]
    kj = jnp.arange(3 * WIN_BLOCK)[None, :]
    key_pos = jnp.arange(nb)[:, None, None] * WIN_BLOCK + kj - WIN_BLOCK
    valid = (jnp.abs(kj - WIN_BLOCK - qi) <= WINDOW) & (key_pos >= 0) & (key_pos < s)
    scale = DH_A ** -0.5
    sink_l = sink.reshape(H_KV, GQA_GROUP)

    def block(args):
        qn, kn, vn, mask = args
        s_ctx = jnp.einsum('bqhgd,bchd->bhgqc', qn, k_ctx) * scale
        s_win = jnp.einsum('bqhgd,bkhd->bhgqk', qn, kn) * scale
        s_win = jnp.where(mask, s_win, -jnp.inf)
        sink_col = jnp.broadcast_to(sink_l[None, :, :, None, None], s_ctx.shape[:-1] + (1,))
        p = jax.nn.softmax(jnp.concatenate([sink_col, s_ctx, s_win], -1), -1)
        return (jnp.einsum('bhgqc,bchd->bqhgd', p[..., 1:1 + lc], v_ctx)
                + jnp.einsum('bhgqk,bkhd->bqhgd', p[..., 1 + lc:], vn))

    out = lax.map(block, (qb, kw, vw, valid))
    return jnp.moveaxis(out, 0, 1).reshape(b, s, H_A, DH_A)


def _context_attention(q, k, v, sink):
    b, lc = q.shape[:2]
    qg = q.reshape(b, lc, H_KV, GQA_GROUP, DH_A)
    sc = jnp.einsum('bqhgd,bkhd->bhgqk', qg, k) * DH_A ** -0.5
    sink_col = jnp.broadcast_to(sink.reshape(H_KV, GQA_GROUP)[None, :, :, None, None], sc.shape[:-1] + (1,))
    p = jax.nn.softmax(jnp.concatenate([sink_col, sc], -1), -1)[..., 1:]
    return jnp.einsum('bhgqk,bkhd->bqhgd', p, v).reshape(b, lc, H_A, DH_A)


def _ab_streams(h, w_in, gate_b):
    q_m, k_m, v_m, o_m, g_m, q_a, k_a, v_a = _split(_proj(h, w_in), AB_WIDTHS)
    b, s = h.shape[:2]
    g = g_m.reshape(b, s, 2, 2, H_M) + gate_b
    mlstm = (_heads(q_m, H_M), _heads(k_m, H_M) * DH_M ** -0.5, _heads(v_m, H_M))
    gates = [(g[:, :, d, 0].transpose(0, 2, 1), jax.nn.log_sigmoid(g[:, :, d, 1]).transpose(0, 2, 1))
             for d in range(2)]
    attn = (q_a.reshape(b, s, H_A, DH_A), k_a.reshape(b, s, H_KV, DH_A), v_a.reshape(b, s, H_KV, DH_A))
    return mlstm, gates, o_m, attn


def _merge_ab(m, o, a, norm_w, w_out):
    hm = _head_norm(m, norm_w) * jax.nn.sigmoid(o)
    cat = jnp.concatenate([hm, a.reshape(a.shape[0], a.shape[1], A_Q)], -1)
    return _proj(cat, w_out)


def _mixer_ab(hl, hc, w_in, gate_b, norm_w, sink, w_out, ctx_out):
    ml, gl, ol, (ql, kl, vl) = _ab_streams(hl, w_in, gate_b)
    mc, gc, oc, (qc, kc, vc) = _ab_streams(hc, w_in, gate_b)
    b = hl.shape[0]
    init = (jnp.zeros((b, H_M, DH_M, DH_M), jnp.float32), jnp.zeros((b, H_M, DH_M), jnp.float32),
            jnp.zeros((b, H_M), jnp.float32))
    outs = [_directional(_mlstm_scan, mc + gc[d], ml + gl[d], init, d == 1) for d in range(2)]
    cos, sin = _axial_rope(hl.shape[1])
    a_lat = _window_attention(_rope(ql, cos, sin), _rope(kl, cos, sin), vl, kc, vc, sink)
    y_lat = _merge_ab(outs[0][1] + outs[1][1], ol, a_lat, norm_w, w_out)
    if not ctx_out:
        return y_lat, None
    y_ctx = _merge_ab(outs[0][0] + outs[1][0], oc, _context_attention(qc, kc, vc, sink), norm_w, w_out)
    return y_lat, y_ctx


def _c_streams(h, w_in, gate_up, gate_b):
    q, k, v, g, low = _split(_proj(h, w_in), C_WIDTHS)
    b, s = h.shape[:2]
    qkv = (_heads(q, H_C) * DK_C ** -0.5, _heads(k, H_C), _heads(v, H_C))
    low = low.reshape(b, s, 2, GATE_RANK)
    log_a = [_heads(jax.nn.log_sigmoid(low[:, :, d] @ gate_up[d] + gate_b[d]) / GATE_TAU, H_C) for d in range(2)]
    return qkv, log_a, g


def _merge_c(h, g, norm_w, w_out):
    return _proj(_head_norm(h, norm_w) * jax.nn.silu(g), w_out)


GLA_SUB = 16


def _gla_body(reverse, q_ref, k_ref, v_ref, cum_ref, o_ref, st_t):
    @pl.when(pl.program_id(1) == 0)
    def _():
        st_t[...] = jnp.zeros_like(st_t)

    bf = jnp.bfloat16
    nt = (((1,), (1,)), ((), ()))
    q, k, v, cum = q_ref[0], k_ref[0], v_ref[0], cum_ref[0]
    vb = v.astype(bf)
    inter = lax.dot_general((q * jnp.exp(cum)).astype(bf), st_t[...].astype(bf), nt,
                            preferred_element_type=jnp.float32)
    row = lax.broadcasted_iota(jnp.int32, (GLA_SUB, DK_C), 0)
    blocks = []
    for i in range(CHUNK // GLA_SUB):
        lo = i * GLA_SUB
        qi, ki, vi, ci = q[lo:lo + GLA_SUB], k[lo:lo + GLA_SUB], v[lo:lo + GLA_SUB], cum[lo:lo + GLA_SUB]
        acc = inter[lo:lo + GLA_SUB]
        if reverse:
            seen, p = slice(lo + GLA_SUB, CHUNK), cum[lo + GLA_SUB - 1:lo + GLA_SUB]
        else:
            seen, p = slice(0, lo), cum[lo:lo + 1]
        if seen.stop > seen.start:
            a = (qi * jnp.exp(ci - p)).astype(bf)
            b = (k[seen] * jnp.exp(p - cum[seen])).astype(bf)
            sc = lax.dot_general(a, b, nt, preferred_element_type=jnp.float32)
            acc = acc + jnp.dot(sc.astype(bf), vb[seen], preferred_element_type=jnp.float32)
        for s in range(GLA_SUB):
            visible = (row <= s) if reverse else (row >= s)
            e = jnp.where(visible, jnp.exp(ci - ci[s:s + 1]), 0.0)
            col = jnp.sum(qi * ki[s:s + 1] * e, axis=1, keepdims=True)
            acc = acc + col * vi[s:s + 1]
        blocks.append(acc)
    o_ref[0] = jnp.concatenate(blocks, axis=0)

    total = cum[0:1] if reverse else cum[CHUNK - 1:CHUNK]
    kd = k * jnp.exp(total - cum)
    pad = jnp.zeros((LANE - CHUNK, DK_C), jnp.float32)
    vpad = jnp.zeros((LANE - CHUNK, DV_C), jnp.float32)
    v_t = jnp.concatenate([v, vpad], axis=0).T
    st_t[...] = st_t[...] * jnp.exp(total) + jnp.dot(v_t.astype(bf), jnp.concatenate([kd, pad], axis=0).astype(bf),
                                                      preferred_element_type=jnp.float32)


def _gla_chunks(q, k, v, cum, reverse):
    n, s = q.shape[:2]
    assert s % CHUNK == 0
    last = s // CHUNK - 1
    chunk_at = (lambda i, c: (i, last - c, 0)) if reverse else (lambda i, c: (i, c, 0))
    kspec = pl.BlockSpec((1, CHUNK, DK_C), chunk_at)
    vspec = pl.BlockSpec((1, CHUNK, DV_C), chunk_at)
    return pl.pallas_call(
        functools.partial(_gla_body, reverse),
        grid=(n, s // CHUNK),
        in_specs=[kspec, kspec, vspec, kspec],
        out_specs=vspec,
        out_shape=jax.ShapeDtypeStruct((n, s, DV_C), jnp.float32),
        scratch_shapes=[pltpu.VMEM((DV_C, DK_C), jnp.float32)],
        compiler_params=pltpu.CompilerParams(dimension_semantics=("arbitrary", "arbitrary")),
        name="gla_chunks",
    )(q, k, v, cum)


def _gla_bidirectional(sc, ac, sl, al):
    lc, ls = sc[0].shape[2], sl[0].shape[2]
    lead = sl[2].shape[:2]

    def seq(ctx_a, lat_a, reverse):
        a = jnp.concatenate([lat_a, ctx_a] if reverse else [ctx_a, lat_a], axis=2)
        return a.reshape((-1,) + a.shape[2:])

    def chunk_cumsum(a, reverse):
        n, s, d = a.shape
        return lax.cumsum(a.reshape(n, s // CHUNK, CHUNK, d), axis=2, reverse=reverse).reshape(n, s, d)

    outs = []
    for d in range(2):
        rev = d == 1
        q, k, v = (seq(c_, l_, rev) for c_, l_ in zip(sc, sl))
        cum = chunk_cumsum(seq(ac[d], al[d], rev), rev)
        outs.append(_gla_chunks(q, k, v, cum, rev).reshape(lead + (-1, DV_C)))
    fwd, bwd = outs
    h_ctx = fwd[:, :, :lc] + bwd[:, :, ls:]
    h_lat = fwd[:, :, lc:] + bwd[:, :, :ls]
    return h_ctx, h_lat


def _mixer_c(hl, hc, w_in, gate_up, gate_b, norm_w, w_out, ctx_out):
    sl, al, gl = _c_streams(hl, w_in, gate_up, gate_b)
    sc, ac, gc = _c_streams(hc, w_in, gate_up, gate_b)
    h_ctx, h_lat = _gla_bidirectional(sc, ac, sl, al)
    y_lat = _merge_c(h_lat, gl, norm_w, w_out)
    if not ctx_out:
        return y_lat, None
    return y_lat, _merge_c(h_ctx, gc, norm_w, w_out)


PEER_GROUP = 8
PEER_RING = 4
PEER_AHEAD = 3
PEER_TB = 256
DMA_QUEUES = 2
PEER_VMEM_BYTES = (PEER_RING * PEER_GROUP * 2 * D_MODEL * PEER_HEADS * PEER_TOPK * 4
                   + 2 * PEER_TB * (2 * D_MODEL + PEER_HEADS * PEER_TOPK) * 4 + (8 << 20))
N_PICK = PEER_HEADS * PEER_TOPK
MXU_ROWS = 16
SQRT_HALF = 0.7071067811865476
UV_ROWS = 2 * D_MODEL // LANE


def _hi_lo_rows(row):
    n = row.shape[1]
    hi = row.astype(jnp.bfloat16).astype(jnp.float32)
    lo = row - hi
    sub = lax.broadcasted_iota(jnp.int32, (MXU_ROWS, n), 0)
    full = jnp.where(sub == 0, jnp.broadcast_to(hi, (MXU_ROWS, n)),
                     jnp.where(sub == 1, jnp.broadcast_to(lo, (MXU_ROWS, n)), 0.0))
    return full.astype(jnp.bfloat16)


def _peer_expert_body(eid_hbm, h_ref, gate_ref, uv_hbm, out_ref, eid_s, *rest):
    bufs, (row_sems, eid_sem) = rest[:PEER_RING], rest[PEER_RING:]
    i = pl.program_id(0)
    nsteps = pl.num_programs(0)
    cur = i % 2
    nxt = 1 - cur
    n_groups = PEER_TB // PEER_GROUP
    ahead_tokens = PEER_AHEAD * PEER_GROUP

    def eid_copies(step, slot):
        blk = jnp.minimum(step, nsteps - 1)
        tail = jnp.minimum(step + 1, nsteps - 1)
        return (pltpu.make_async_copy(eid_hbm.at[blk], eid_s.at[slot, pl.ds(0, PEER_TB)], eid_sem),
                pltpu.make_async_copy(eid_hbm.at[tail, pl.ds(0, ahead_tokens)],
                                      eid_s.at[slot, pl.ds(PEER_TB, ahead_tokens)], eid_sem))

    def issue_token(base, k, which):
        for e in range(N_PICK):
            pltpu.make_async_copy(uv_hbm.at[eid_s[cur, base + k, e]], bufs[which].at[k, :, e, :],
                                  row_sems.at[which]).start(priority=e % DMA_QUEUES)

    def issue(base, which):
        for k in range(PEER_GROUP):
            issue_token(base, k, which)

    def wait_rows(which):
        pltpu.make_async_copy(bufs[which], bufs[which], row_sems.at[which]).wait()

    @pl.when(i == 0)
    def _():
        for c in eid_copies(0, 0):
            c.start()
        for c in eid_copies(0, 0):
            c.wait()
        for g in range(PEER_AHEAD):
            issue(g * PEER_GROUP, g)

    for c in eid_copies(i + 1, nxt):
        c.start()

    n_chunk = D_MODEL // LANE

    def compute(base, which):
        buf = bufs[which]
        base = pl.multiple_of(base, PEER_GROUP)
        h8 = h_ref[pl.ds(base, PEER_GROUP), :]
        g8 = gate_ref[pl.ds(base, PEER_GROUP), :]
        rows = []
        for k in range(PEER_GROUP):
            issue_token(base + ahead_tokens, k, (which + PEER_AHEAD) % PEER_RING)
            gu = jnp.concatenate([buf[k, s] for s in range(n_chunk)], axis=1).astype(jnp.bfloat16)
            z2 = lax.dot_general(_hi_lo_rows(h8[k:k + 1]), gu, (((1,), (1,)), ((), ())),
                                 preferred_element_type=jnp.float32)
            z = z2[0:1] + z2[1:2]
            act = 0.5 * z * (1.0 + lax.erf(z * SQRT_HALF))
            gv = jnp.concatenate([buf[k, n_chunk + s] for s in range(n_chunk)], axis=1).astype(jnp.bfloat16)
            o2 = jnp.dot(_hi_lo_rows(g8[k:k + 1] * act), gv, preferred_element_type=jnp.float32)
            rows.append(o2[0:1] + o2[1:2])
        out_ref[pl.ds(base, PEER_GROUP), :] = jnp.concatenate(rows, axis=0)

    def ring_turn(j, carry):
        for w in range(PEER_RING):
            wait_rows(w)
            compute((j * PEER_RING + w) * PEER_GROUP, w)
        return carry

    lax.fori_loop(0, n_groups // PEER_RING, ring_turn, 0)
    for c in eid_copies(i + 1, nxt):
        c.wait()

    @pl.when(i == nsteps - 1)
    def _():
        for g in range(PEER_AHEAD):
            wait_rows(g)


def _peer_experts(h, eid, gate, uv):
    t = h.shape[0]
    assert t % PEER_TB == 0 and PEER_TB % (PEER_RING * PEER_GROUP) == 0 and 0 < PEER_AHEAD < PEER_RING
    nb = t // PEER_TB
    group_buf = pltpu.VMEM((PEER_GROUP, UV_ROWS, N_PICK, LANE), jnp.float32)
    return pl.pallas_call(
        _peer_expert_body,
        grid=(nb,),
        in_specs=[pl.BlockSpec(memory_space=pl.ANY),
                  pl.BlockSpec((PEER_TB, D_MODEL), lambda i: (i, 0)),
                  pl.BlockSpec((PEER_TB, N_PICK), lambda i: (i, 0)),
                  pl.BlockSpec(memory_space=pl.ANY)],
        out_specs=pl.BlockSpec((PEER_TB, D_MODEL), lambda i: (i, 0)),
        out_shape=jax.ShapeDtypeStruct((t, D_MODEL), jnp.float32),
        scratch_shapes=[pltpu.SMEM((2, PEER_TB + PEER_AHEAD * PEER_GROUP, N_PICK), jnp.int32)]
        + [group_buf] * PEER_RING
        + [pltpu.SemaphoreType.DMA((PEER_RING,)), pltpu.SemaphoreType.DMA],
        compiler_params=pltpu.CompilerParams(dimension_semantics=("arbitrary",),
                                             vmem_limit_bytes=PEER_VMEM_BYTES),
        name="peer_experts",
    )(eid.reshape(nb, PEER_TB, N_PICK), h, gate, uv)


SEL_TB = 256
SEL_NARROW = 4
RANK_NONE = 1e9


def _top_rows(s, ids, k, row=None):
    if row is None:
        row = lax.broadcasted_iota(jnp.int32, s.shape, 0).astype(jnp.float32)
    vals, picks = [], []
    for _ in range(k):
        m = jnp.max(s, axis=0, keepdims=True)
        first = jnp.min(jnp.where(s == m, row, RANK_NONE), axis=0, keepdims=True)
        hit = row == first
        vals.append(m)
        picks.append(first if ids is None else jnp.sum(jnp.where(hit, ids, 0.0), axis=0, keepdims=True))
        s = jnp.where(hit, -jnp.inf, s)
    return jnp.concatenate(vals, axis=0), jnp.concatenate(picks, axis=0)


def _peer_select_body(h_ref, wq_ref, keys_ref, eid_ref, gate_ref, q_s, eid_t, gate_t):
    q_s[...] = jnp.dot(h_ref[...].astype(jnp.bfloat16), wq_ref[...], preferred_element_type=jnp.float32)

    def head(hd, carry):
        tops = []
        for p in range(2):
            col = pl.multiple_of((hd * 2 + p) * PEER_HALF, PEER_HALF)
            q_hp = q_s[:, pl.ds(col, PEER_HALF)].astype(jnp.bfloat16)
            s = lax.dot_general(keys_ref[hd * 2 + p], q_hp, (((1,), (1,)), ((), ())),
                                preferred_element_type=jnp.float32)
            tops.append(_top_rows(s, None, PEER_TOPK))
        (v0, i0), (v1, i1) = tops
        kf = float(PEER_TOPK)
        r16 = lax.broadcasted_iota(jnp.int32, v0.shape, 0).astype(jnp.float32)
        v0_rest = jnp.where(r16 >= float(SEL_NARROW), v0, -jnp.inf)
        cand = jnp.concatenate([v0[i:i + 1] + v1 for i in range(SEL_NARROW)]
                               + [v0_rest + v1[j:j + 1] for j in range(SEL_NARROW)], axis=0)
        rank = jnp.concatenate([i * kf + r16 for i in range(SEL_NARROW)]
                               + [jnp.where(r16 >= float(SEL_NARROW), r16 * kf + j, RANK_NONE)
                                  for j in range(SEL_NARROW)], axis=0)
        cid = jnp.concatenate([i0[i:i + 1] * float(N_KEYS) + i1 for i in range(SEL_NARROW)]
                              + [i0 * float(N_KEYS) + i1[j:j + 1] for j in range(SEL_NARROW)], axis=0)
        best, eid = _top_rows(cand, cid, PEER_TOPK, rank)
        ex = jnp.exp(best - best[0:1])
        row0 = pl.multiple_of(hd * PEER_TOPK, PEER_TOPK)
        eid_t[pl.ds(row0, PEER_TOPK), :] = eid
        gate_t[pl.ds(row0, PEER_TOPK), :] = ex / jnp.sum(ex, axis=0, keepdims=True)
        return carry

    lax.fori_loop(0, PEER_HEADS, head, 0)
    eid_ref[...] = eid_t[...].T.astype(jnp.int32)
    gate_ref[...] = gate_t[...].T


def _peer_select(h, wq, keys):
    t = h.shape[0]
    assert t % SEL_TB == 0
    qw = PEER_HEADS * PEER_KEY_DIM
    return pl.pallas_call(
        _peer_select_body,
        grid=(t // SEL_TB,),
        in_specs=[pl.BlockSpec((SEL_TB, D_MODEL), lambda i: (i, 0)),
                  pl.BlockSpec((D_MODEL, qw), lambda i: (0, 0)),
                  pl.BlockSpec((PEER_HEADS * 2, N_KEYS, PEER_HALF), lambda i: (0, 0, 0))],
        out_specs=[pl.BlockSpec((SEL_TB, N_PICK), lambda i: (i, 0)),
                   pl.BlockSpec((SEL_TB, N_PICK), lambda i: (i, 0))],
        out_shape=[jax.ShapeDtypeStruct((t, N_PICK), jnp.int32),
                   jax.ShapeDtypeStruct((t, N_PICK), jnp.float32)],
        scratch_shapes=[pltpu.VMEM((SEL_TB, qw), jnp.float32),
                        pltpu.VMEM((N_PICK, SEL_TB), jnp.float32),
                        pltpu.VMEM((N_PICK, SEL_TB), jnp.float32)],
        name="peer_select",
    )(h, wq.astype(jnp.bfloat16), keys.reshape(PEER_HEADS * 2, N_KEYS, PEER_HALF).astype(jnp.bfloat16))


def _peer(h, wq, keys, uv):
    shape = h.shape
    h2 = h.reshape(-1, shape[-1])
    eid, gate = _peer_select(h2, wq, keys)
    return _peer_experts(h2, eid, gate, uv).reshape(shape)


def kernel(x, c, ctx, c_ctx, w_mod, b_mod, ln_w, ln_b, ab_w_in, ab_gate_b, ab_norm_w, ab_sink, ab_w_out,
           gla_w_in, gla_gate_up, gla_gate_b, gla_norm_w, gla_w_out, peer_wq, peer_keys, peer_u, peer_v):
    for layer in range(DEPTH):
        last = layer == DEPTH - 1
        j = layer // 2
        mod_l = jax.nn.silu(c) @ w_mod[layer] + b_mod[layer]
        mod_c = jax.nn.silu(c_ctx) @ w_mod[layer] + b_mod[layer]
        sh1, sc1, g1, sh2, sc2, g2 = jnp.split(mod_l[:, None, :], 6, axis=-1)
        csh1, csc1, cg1, csh2, csc2, cg2 = jnp.split(mod_c, 6)
        hl = x * (1.0 + sc1) + sh1
        hc = ctx * (1.0 + csc1) + csh1
        if layer % 2 == 0:
            y_lat, y_ctx = _mixer_ab(hl, hc, ab_w_in[j], ab_gate_b[j], ab_norm_w[j], ab_sink[j], ab_w_out[j], not last)
        else:
            y_lat, y_ctx = _mixer_c(hl, hc, gla_w_in[j], gla_gate_up[j], gla_gate_b[j], gla_norm_w[j], gla_w_out[j],
                                    not last)
        x = _layernorm(DEEPNORM_ALPHA * x + g1 * y_lat, ln_w[layer, 0], ln_b[layer, 0])
        uv = jnp.concatenate([peer_u[layer].reshape(-1, UV_ROWS // 2, LANE),
                              peer_v[layer].reshape(-1, UV_ROWS // 2, LANE)], axis=1)
        f_lat = _peer(x * (1.0 + sc2) + sh2, peer_wq[layer], peer_keys[layer], uv)
        x = _layernorm(DEEPNORM_ALPHA * x + g2 * f_lat, ln_w[layer, 1], ln_b[layer, 1])
        if not last:
            ctx = _layernorm(DEEPNORM_ALPHA * ctx + cg1 * y_ctx, ln_w[layer, 0], ln_b[layer, 0])
            f_ctx = _peer(ctx * (1.0 + csc2) + csh2, peer_wq[layer], peer_keys[layer], uv)
            ctx = _layernorm(DEEPNORM_ALPHA * ctx + cg2 * f_ctx, ln_w[layer, 1], ln_b[layer, 1])
    return x
```

```python
import functools

import jax
import jax.numpy as jnp
import numpy as np
from jax import lax
from jax.experimental import pallas as pl
from jax.experimental.pallas import tpu as pltpu

D_MODEL = 1024
DEPTH = 2
GRID_W = 64
H_M = 4
DH_M = 128
M_W = H_M * DH_M
H_A = 8
H_KV = 2
DH_A = 64
A_Q = H_A * DH_A
A_KV = H_KV * DH_A
GQA_GROUP = H_A // H_KV
WINDOW = 128
WIN_BLOCK = 128
ROPE_THETA = 10000.0
AB_WIDTHS = (M_W, M_W, M_W, M_W, 4 * H_M, A_Q, A_KV, A_KV)
MIX_W = M_W + A_Q
H_C = 4
DK_C = 128
DV_C = 256
C_K = H_C * DK_C
C_V = H_C * DV_C
GATE_RANK = 16
GATE_TAU = 16.0
C_WIDTHS = (C_K, C_K, C_V, C_V, 2 * GATE_RANK)
CHUNK = 64
N_KEYS = 128
PEER_HEADS = 8
PEER_TOPK = 16
PEER_KEY_DIM = 256
PEER_HALF = PEER_KEY_DIM // 2
PEER_CHUNK = 128
DEEPNORM_ALPHA = (2 * DEPTH) ** 0.25
LN_EPS = 1e-5

LANE = 128
MM_TILE_M = 1024
MM_TILE_N = 512


def _matmul_body(x_ref, w_ref, o_ref):
    o_ref[...] = jnp.dot(x_ref[...].astype(jnp.bfloat16), w_ref[...], preferred_element_type=jnp.float32)


def _matmul(x, w):
    m, k = x.shape
    n = w.shape[1]
    n_pad = -n % MM_TILE_N
    w = w.astype(jnp.bfloat16)
    if n_pad:
        w = jnp.pad(w, ((0, 0), (0, n_pad)))
    tm = min(MM_TILE_M, m)
    assert m % tm == 0
    out = pl.pallas_call(
        _matmul_body,
        grid=((n + n_pad) // MM_TILE_N, m // tm),
        in_specs=[pl.BlockSpec((tm, k), lambda j, i: (i, 0)),
                  pl.BlockSpec((k, MM_TILE_N), lambda j, i: (0, j))],
        out_specs=pl.BlockSpec((tm, MM_TILE_N), lambda j, i: (i, j)),
        out_shape=jax.ShapeDtypeStruct((m, n + n_pad), jnp.float32),
        name="proj_matmul",
    )(x, w)
    return out[:, :n] if n_pad else out


def _proj(h, w):
    lead = h.shape[:-1]
    return _matmul(h.reshape(-1, h.shape[-1]), w).reshape(lead + (w.shape[1],))


def _layernorm(x, w, b):
    mu = x.mean(-1, keepdims=True)
    var = jnp.mean(jnp.square(x - mu), -1, keepdims=True)
    return ((x - mu) * lax.rsqrt(var + LN_EPS)) * w + b


def _split(a, widths):
    idx = np.cumsum(widths)[:-1].tolist()
    return jnp.split(a, idx, axis=-1)


def _heads(a, h):
    b, s = a.shape[:2]
    return a.reshape(b, s, h, -1).transpose(0, 2, 1, 3)


def _head_norm(h, w):
    hf = h * lax.rsqrt(jnp.mean(h * h, -1, keepdims=True) + LN_EPS)
    b, nh, s, d = h.shape
    return hf.transpose(0, 2, 1, 3).reshape(b, s, nh * d) * w


def _to_chunks(a):
    b, h, s = a.shape[:3]
    a = a.reshape((b, h, s // CHUNK, CHUNK) + a.shape[3:])
    return jnp.moveaxis(a, 2, 0)


def _from_chunks(a):
    a = jnp.moveaxis(a, 0, 2)
    return a.reshape(a.shape[:2] + (-1,) + a.shape[4:])


def _mlstm_scan(q, k, v, log_i, log_f, state):
    causal = jnp.tril(jnp.ones((CHUNK, CHUNK), bool))

    def step(carry, inp):
        c_st, n_st, m_st = carry
        qc, kc, vc, li, lf = inp
        cum = jnp.cumsum(lf, axis=-1)
        dmat = cum[..., :, None] - cum[..., None, :] + li[..., None, :]
        dmat = jnp.where(causal, dmat, -jnp.inf)
        m_inter = cum + m_st[..., None]
        m_t = jnp.maximum(m_inter, dmat.max(-1))
        w = jnp.exp(dmat - m_t[..., None]) * jnp.einsum('bhtd,bhsd->bhts', qc, kc)
        a_inter = jnp.exp(m_inter - m_t)
        num = a_inter[..., None] * jnp.einsum('bhtd,bhde->bhte', qc, c_st) + jnp.einsum('bhts,bhse->bhte', w, vc)
        den = a_inter * jnp.einsum('bhtd,bhd->bht', qc, n_st) + w.sum(-1)
        h = num / jnp.maximum(jnp.abs(den), jnp.exp(-m_t))[..., None]
        total = cum[..., -1]
        decay_s = total[..., None] - cum + li
        m_new = jnp.maximum(total + m_st, decay_s.max(-1))
        ws = jnp.exp(decay_s - m_new[..., None])
        a_st = jnp.exp(total + m_st - m_new)
        c_st = a_st[..., None, None] * c_st + jnp.einsum('bhs,bhsd,bhse->bhde', ws, kc, vc)
        n_st = a_st[..., None] * n_st + jnp.einsum('bhs,bhsd->bhd', ws, kc)
        return (c_st, n_st, m_new), h

    state, hs = lax.scan(step, state, tuple(_to_chunks(a) for a in (q, k, v, log_i, log_f)))
    return _from_chunks(hs), state


def _gla_scan(q, k, v, log_a, state):
    causal = jnp.tril(jnp.ones((CHUNK, CHUNK), bool))

    def step(st, inp):
        qc, kc, vc, la = inp
        cum = jnp.cumsum(la, axis=2)
        rel = cum[:, :, :, None, :] - cum[:, :, None, :, :]
        rel = jnp.where(causal[:, :, None], rel, -jnp.inf)
        scores = jnp.einsum('bhtk,bhsk,bhtsk->bhts', qc, kc, jnp.exp(rel))
        out = jnp.einsum('bhtk,bhkv->bhtv', qc * jnp.exp(cum), st) + jnp.einsum('bhts,bhsv->bhtv', scores, vc)
        total = cum[:, :, -1:, :]
        st = jnp.exp(total[:, :, 0, :, None]) * st + jnp.einsum('bhsk,bhsv->bhkv', kc * jnp.exp(total - cum), vc)
        return st, out

    state, outs = lax.scan(step, state, tuple(_to_chunks(a) for a in (q, k, v, log_a)))
    return _from_chunks(outs), state


def _directional(scan_fn, ctx_seq, lat_seq, init, reverse):
    if reverse:
        ctx_seq = tuple(jnp.flip(a, 2) for a in ctx_seq)
        lat_seq = tuple(jnp.flip(a, 2) for a in lat_seq)
    h_ctx, state = scan_fn(*ctx_seq, init)
    h_lat, _ = scan_fn(*lat_seq, state)
    if reverse:
        h_ctx, h_lat = jnp.flip(h_ctx, 2), jnp.flip(h_lat, 2)
    return h_ctx, h_lat


def _axial_rope(s):
    rows = s // GRID_W
    row = jnp.repeat(jnp.arange(rows), GRID_W).astype(jnp.float32)
    col = jnp.tile(jnp.arange(GRID_W), rows).astype(jnp.float32)
    n_freq = DH_A // 4
    inv = ROPE_THETA ** (-jnp.arange(n_freq, dtype=jnp.float32) / n_freq)
    ang = jnp.concatenate([row[:, None] * inv, col[:, None] * inv], -1)
    return jnp.cos(ang), jnp.sin(ang)


def _rope(x, cos, sin):
    x1, x2 = x[..., 0::2], x[..., 1::2]
    c = cos[None, :, None, :]
    sn = sin[None, :, None, :]
    return jnp.stack([x1 * c - x2 * sn, x1 * sn + x2 * c], -1).reshape(x.shape)


def _window_attention(q, k, v, k_ctx, v_ctx, sink):
    b, s = q.shape[:2]
    nb = s // WIN_BLOCK
    lc = k_ctx.shape[1]
    qb = jnp.moveaxis(q.reshape(b, nb, WIN_BLOCK, H_KV, GQA_GROUP, DH_A), 1, 0)

    def bands(a):
        ap = jnp.pad(a, ((0, 0), (WIN_BLOCK, WIN_BLOCK), (0, 0), (0, 0))).reshape(b, nb + 2, WIN_BLOCK, H_KV, DH_A)
        return jnp.moveaxis(jnp.concatenate([ap[:, :-2], ap[:, 1:-1], ap[:, 2:]], 2), 1, 0)

    kw, vw = bands(k), bands(v)
    qi = jnp.arange(WIN_BLOCK)[:, None]
    kj = jnp.arange(3 * WIN_BLOCK)[None, :]
    key_pos = jnp.arange(nb)[:, None, None] * WIN_BLOCK + kj - WIN_BLOCK
    valid = (jnp.abs(kj - WIN_BLOCK - qi) <= WINDOW) & (key_pos >= 0) & (key_pos < s)
    scale = DH_A ** -0.5
    sink_l = sink.reshape(H_KV, GQA_GROUP)

    def block(args):
        qn, kn, vn, mask = args
        s_ctx = jnp.einsum('bqhgd,bchd->bhgqc', qn, k_ctx) * scale
        s_win = jnp.einsum('bqhgd,bkhd->bhgqk', qn, kn) * scale
        s_win = jnp.where(mask, s_win, -jnp.inf)
        sink_col = jnp.broadcast_to(sink_l[None, :, :, None, None], s_ctx.shape[:-1] + (1,))
        p = jax.nn.softmax(jnp.concatenate([sink_col, s_ctx, s_win], -1), -1)
        return (jnp.einsum('bhgqc,bchd->bqhgd', p[..., 1:1 + lc], v_ctx)
                + jnp.einsum('bhgqk,bkhd->bqhgd', p[..., 1 + lc:], vn))

    out = lax.map(block, (qb, kw, vw, valid))
    return jnp.moveaxis(out, 0, 1).reshape(b, s, H_A, DH_A)


def _context_attention(q, k, v, sink):
    b, lc = q.shape[:2]
    qg = q.reshape(b, lc, H_KV, GQA_GROUP, DH_A)
    sc = jnp.einsum('bqhgd,bkhd->bhgqk', qg, k) * DH_A ** -0.5
    sink_col = jnp.broadcast_to(sink.reshape(H_KV, GQA_GROUP)[None, :, :, None, None], sc.shape[:-1] + (1,))
    p = jax.nn.softmax(jnp.concatenate([sink_col, sc], -1), -1)[..., 1:]
    return jnp.einsum('bhgqk,bkhd->bqhgd', p, v).reshape(b, lc, H_A, DH_A)


def _ab_streams(h, w_in, gate_b):
    q_m, k_m, v_m, o_m, g_m, q_a, k_a, v_a = _split(_proj(h, w_in), AB_WIDTHS)
    b, s = h.shape[:2]
    g = g_m.reshape(b, s, 2, 2, H_M) + gate_b
    mlstm = (_heads(q_m, H_M), _heads(k_m, H_M) * DH_M ** -0.5, _heads(v_m, H_M))
    gates = [(g[:, :, d, 0].transpose(0, 2, 1), jax.nn.log_sigmoid(g[:, :, d, 1]).transpose(0, 2, 1))
             for d in range(2)]
    attn = (q_a.reshape(b, s, H_A, DH_A), k_a.reshape(b, s, H_KV, DH_A), v_a.reshape(b, s, H_KV, DH_A))
    return mlstm, gates, o_m, attn


def _merge_ab(m, o, a, norm_w, w_out):
    hm = _head_norm(m, norm_w) * jax.nn.sigmoid(o)
    cat = jnp.concatenate([hm, a.reshape(a.shape[0], a.shape[1], A_Q)], -1)
    return _proj(cat, w_out)


def _mixer_ab(hl, hc, w_in, gate_b, norm_w, sink, w_out, ctx_out):
    ml, gl, ol, (ql, kl, vl) = _ab_streams(hl, w_in, gate_b)
    mc, gc, oc, (qc, kc, vc) = _ab_streams(hc, w_in, gate_b)
    b = hl.shape[0]
    init = (jnp.zeros((b, H_M, DH_M, DH_M), jnp.float32), jnp.zeros((b, H_M, DH_M), jnp.float32),
            jnp.zeros((b, H_M), jnp.float32))
    outs = [_directional(_mlstm_scan, mc + gc[d], ml + gl[d], init, d == 1) for d in range(2)]
    cos, sin = _axial_rope(hl.shape[1])
    a_lat = _window_attention(_rope(ql, cos, sin), _rope(kl, cos, sin), vl, kc, vc, sink)
    y_lat = _merge_ab(outs[0][1] + outs[1][1], ol, a_lat, norm_w, w_out)
    if not ctx_out:
        return y_lat, None
    y_ctx = _merge_ab(outs[0][0] + outs[1][0], oc, _context_attention(qc, kc, vc, sink), norm_w, w_out)
    return y_lat, y_ctx


def _c_streams(h, w_in, gate_up, gate_b):
    q, k, v, g, low = _split(_proj(h, w_in), C_WIDTHS)
    b, s = h.shape[:2]
    qkv = (_heads(q, H_C) * DK_C ** -0.5, _heads(k, H_C), _heads(v, H_C))
    low = low.reshape(b, s, 2, GATE_RANK)
    log_a = [_heads(jax.nn.log_sigmoid(low[:, :, d] @ gate_up[d] + gate_b[d]) / GATE_TAU, H_C) for d in range(2)]
    return qkv, log_a, g


def _merge_c(h, g, norm_w, w_out):
    return _proj(_head_norm(h, norm_w) * jax.nn.silu(g), w_out)


GLA_SUB = 16


def _gla_body(reverse, q_ref, k_ref, v_ref, cum_ref, o_ref, st_t):
    @pl.when(pl.program_id(1) == 0)
    def _():
        st_t[...] = jnp.zeros_like(st_t)

    bf = jnp.bfloat16
    nt = (((1,), (1,)), ((), ()))
    q, k, v, cum = q_ref[0], k_ref[0], v_ref[0], cum_ref[0]
    vb = v.astype(bf)
    inter = lax.dot_general((q * jnp.exp(cum)).astype(bf), st_t[...].astype(bf), nt,
                            preferred_element_type=jnp.float32)
    row = lax.broadcasted_iota(jnp.int32, (GLA_SUB, DK_C), 0)
    blocks = []
    for i in range(CHUNK // GLA_SUB):
        lo = i * GLA_SUB
        qi, ki, vi, ci = q[lo:lo + GLA_SUB], k[lo:lo + GLA_SUB], v[lo:lo + GLA_SUB], cum[lo:lo + GLA_SUB]
        acc = inter[lo:lo + GLA_SUB]
        if reverse:
            seen, p = slice(lo + GLA_SUB, CHUNK), cum[lo + GLA_SUB - 1:lo + GLA_SUB]
        else:
            seen, p = slice(0, lo), cum[lo:lo + 1]
        if seen.stop > seen.start:
            a = (qi * jnp.exp(ci - p)).astype(bf)
            b = (k[seen] * jnp.exp(p - cum[seen])).astype(bf)
            sc = lax.dot_general(a, b, nt, preferred_element_type=jnp.float32)
            acc = acc + jnp.dot(sc.astype(bf), vb[seen], preferred_element_type=jnp.float32)
        for s in range(GLA_SUB):
            visible = (row <= s) if reverse else (row >= s)
            e = jnp.where(visible, jnp.exp(ci - ci[s:s + 1]), 0.0)
            col = jnp.sum(qi * ki[s:s + 1] * e, axis=1, keepdims=True)
            acc = acc + col * vi[s:s + 1]
        blocks.append(acc)
    o_ref[0] = jnp.concatenate(blocks, axis=0)

    total = cum[0:1] if reverse else cum[CHUNK - 1:CHUNK]
    kd = k * jnp.exp(total - cum)
    pad = jnp.zeros((LANE - CHUNK, DK_C), jnp.float32)
    vpad = jnp.zeros((LANE - CHUNK, DV_C), jnp.float32)
    v_t = jnp.concatenate([v, vpad], axis=0).T
    st_t[...] = st_t[...] * jnp.exp(total) + jnp.dot(v_t.astype(bf), jnp.concatenate([kd, pad], axis=0).astype(bf),
                                                      preferred_element_type=jnp.float32)


def _gla_chunks(q, k, v, cum, reverse):
    n, s = q.shape[:2]
    assert s % CHUNK == 0
    last = s // CHUNK - 1
    chunk_at = (lambda i, c: (i, last - c, 0)) if reverse else (lambda i, c: (i, c, 0))
    kspec = pl.BlockSpec((1, CHUNK, DK_C), chunk_at)
    vspec = pl.BlockSpec((1, CHUNK, DV_C), chunk_at)
    return pl.pallas_call(
        functools.partial(_gla_body, reverse),
        grid=(n, s // CHUNK),
        in_specs=[kspec, kspec, vspec, kspec],
        out_specs=vspec,
        out_shape=jax.ShapeDtypeStruct((n, s, DV_C), jnp.float32),
        scratch_shapes=[pltpu.VMEM((DV_C, DK_C), jnp.float32)],
        compiler_params=pltpu.CompilerParams(dimension_semantics=("arbitrary", "arbitrary")),
        name="gla_chunks",
    )(q, k, v, cum)


def _gla_bidirectional(sc, ac, sl, al):
    lc, ls = sc[0].shape[2], sl[0].shape[2]
    lead = sl[2].shape[:2]

    def seq(ctx_a, lat_a, reverse):
        a = jnp.concatenate([lat_a, ctx_a] if reverse else [ctx_a, lat_a], axis=2)
        return a.reshape((-1,) + a.shape[2:])

    def chunk_cumsum(a, reverse):
        n, s, d = a.shape
        return lax.cumsum(a.reshape(n, s // CHUNK, CHUNK, d), axis=2, reverse=reverse).reshape(n, s, d)

    outs = []
    for d in range(2):
        rev = d == 1
        q, k, v = (seq(c_, l_, rev) for c_, l_ in zip(sc, sl))
        cum = chunk_cumsum(seq(ac[d], al[d], rev), rev)
        outs.append(_gla_chunks(q, k, v, cum, rev).reshape(lead + (-1, DV_C)))
    fwd, bwd = outs
    h_ctx = fwd[:, :, :lc] + bwd[:, :, ls:]
    h_lat = fwd[:, :, lc:] + bwd[:, :, :ls]
    return h_ctx, h_lat


def _mixer_c(hl, hc, w_in, gate_up, gate_b, norm_w, w_out, ctx_out):
    sl, al, gl = _c_streams(hl, w_in, gate_up, gate_b)
    sc, ac, gc = _c_streams(hc, w_in, gate_up, gate_b)
    h_ctx, h_lat = _gla_bidirectional(sc, ac, sl, al)
    y_lat = _merge_c(h_lat, gl, norm_w, w_out)
    if not ctx_out:
        return y_lat, None
    return y_lat, _merge_c(h_ctx, gc, norm_w, w_out)


PEER_GROUP = 8
PEER_RING = 4
PEER_AHEAD = 3
PEER_TB = 256
DMA_QUEUES = 2
PEER_VMEM_BYTES = (PEER_RING * PEER_GROUP * 2 * D_MODEL * PEER_HEADS * PEER_TOPK * 4
                   + 2 * PEER_TB * (2 * D_MODEL + PEER_HEADS * PEER_TOPK) * 4 + (8 << 20))
N_PICK = PEER_HEADS * PEER_TOPK
MXU_ROWS = 16
SQRT_HALF = 0.7071067811865476
UV_ROWS = 2 * D_MODEL // LANE


def _hi_lo_rows(row):
    n = row.shape[1]
    hi = row.astype(jnp.bfloat16).astype(jnp.float32)
    lo = row - hi
    sub = lax.broadcasted_iota(jnp.int32, (MXU_ROWS, n), 0)
    full = jnp.where(sub == 0, jnp.broadcast_to(hi, (MXU_ROWS, n)),
                     jnp.where(sub == 1, jnp.broadcast_to(lo, (MXU_ROWS, n)), 0.0))
    return full.astype(jnp.bfloat16)


def _peer_expert_body(eid_hbm, h_ref, gate_ref, uv_hbm, out_ref, eid_s, *rest):
    bufs, (row_sems, eid_sem) = rest[:PEER_RING], rest[PEER_RING:]
    i = pl.program_id(0)
    nsteps = pl.num_programs(0)
    cur = i % 2
    nxt = 1 - cur
    n_groups = PEER_TB // PEER_GROUP
    ahead_tokens = PEER_AHEAD * PEER_GROUP

    def eid_copies(step, slot):
        blk = jnp.minimum(step, nsteps - 1)
        tail = jnp.minimum(step + 1, nsteps - 1)
        return (pltpu.make_async_copy(eid_hbm.at[blk], eid_s.at[slot, pl.ds(0, PEER_TB)], eid_sem),
                pltpu.make_async_copy(eid_hbm.at[tail, pl.ds(0, ahead_tokens)],
                                      eid_s.at[slot, pl.ds(PEER_TB, ahead_tokens)], eid_sem))

    def issue_token(base, k, which):
        for e in range(N_PICK):
            pltpu.make_async_copy(uv_hbm.at[eid_s[cur, base + k, e]], bufs[which].at[k, :, e, :],
                                  row_sems.at[which]).start(priority=e % DMA_QUEUES)

    def issue(base, which):
        for k in range(PEER_GROUP):
            issue_token(base, k, which)

    def wait_rows(which):
        pltpu.make_async_copy(bufs[which], bufs[which], row_sems.at[which]).wait()

    @pl.when(i == 0)
    def _():
        for c in eid_copies(0, 0):
            c.start()
        for c in eid_copies(0, 0):
            c.wait()
        for g in range(PEER_AHEAD):
            issue(g * PEER_GROUP, g)

    for c in eid_copies(i + 1, nxt):
        c.start()

    n_chunk = D_MODEL // LANE

    def compute(base, which):
        buf = bufs[which]
        base = pl.multiple_of(base, PEER_GROUP)
        h8 = h_ref[pl.ds(base, PEER_GROUP), :]
        g8 = gate_ref[pl.ds(base, PEER_GROUP), :]
        rows = []
        for k in range(PEER_GROUP):
            issue_token(base + ahead_tokens, k, (which + PEER_AHEAD) % PEER_RING)
            gu = jnp.concatenate([buf[k, s] for s in range(n_chunk)], axis=1).astype(jnp.bfloat16)
            z2 = lax.dot_general(_hi_lo_rows(h8[k:k + 1]), gu, (((1,), (1,)), ((), ())),
                                 preferred_element_type=jnp.float32)
            z = z2[0:1] + z2[1:2]
            act = 0.5 * z * (1.0 + lax.erf(z * SQRT_HALF))
            gv = jnp.concatenate([buf[k, n_chunk + s] for s in range(n_chunk)], axis=1).astype(jnp.bfloat16)
            o2 = jnp.dot(_hi_lo_rows(g8[k:k + 1] * act), gv, preferred_element_type=jnp.float32)
            rows.append(o2[0:1] + o2[1:2])
        out_ref[pl.ds(base, PEER_GROUP), :] = jnp.concatenate(rows, axis=0)

    def ring_turn(j, carry):
        for w in range(PEER_RING):
            wait_rows(w)
            compute((j * PEER_RING + w) * PEER_GROUP, w)
        return carry

    lax.fori_loop(0, n_groups // PEER_RING, ring_turn, 0)
    for c in eid_copies(i + 1, nxt):
        c.wait()

    @pl.when(i == nsteps - 1)
    def _():
        for g in range(PEER_AHEAD):
            wait_rows(g)


def _peer_experts(h, eid, gate, uv):
    t = h.shape[0]
    assert t % PEER_TB == 0 and PEER_TB % (PEER_RING * PEER_GROUP) == 0 and 0 < PEER_AHEAD < PEER_RING
    nb = t // PEER_TB
    group_buf = pltpu.VMEM((PEER_GROUP, UV_ROWS, N_PICK, LANE), jnp.float32)
    return pl.pallas_call(
        _peer_expert_body,
        grid=(nb,),
        in_specs=[pl.BlockSpec(memory_space=pl.ANY),
                  pl.BlockSpec((PEER_TB, D_MODEL), lambda i: (i, 0)),
                  pl.BlockSpec((PEER_TB, N_PICK), lambda i: (i, 0)),
                  pl.BlockSpec(memory_space=pl.ANY)],
        out_specs=pl.BlockSpec((PEER_TB, D_MODEL), lambda i: (i, 0)),
        out_shape=jax.ShapeDtypeStruct((t, D_MODEL), jnp.float32),
        scratch_shapes=[pltpu.SMEM((2, PEER_TB + PEER_AHEAD * PEER_GROUP, N_PICK), jnp.int32)]
        + [group_buf] * PEER_RING
        + [pltpu.SemaphoreType.DMA((PEER_RING,)), pltpu.SemaphoreType.DMA],
        compiler_params=pltpu.CompilerParams(dimension_semantics=("arbitrary",),
                                             vmem_limit_bytes=PEER_VMEM_BYTES),
        name="peer_experts",
    )(eid.reshape(nb, PEER_TB, N_PICK), h, gate, uv)


SEL_TB = 512
SEL_NARROW = 4
RANK_NONE = 1e9


def _top_rows(s, ids, k, row=None):
    if row is None:
        row = lax.broadcasted_iota(jnp.int32, s.shape, 0).astype(jnp.float32)
    vals, picks = [], []
    for _ in range(k):
        m = jnp.max(s, axis=0, keepdims=True)
        first = jnp.min(jnp.where(s == m, row, RANK_NONE), axis=0, keepdims=True)
        hit = row == first
        vals.append(m)
        picks.append(first if ids is None else jnp.sum(jnp.where(hit, ids, 0.0), axis=0, keepdims=True))
        s = jnp.where(hit, -jnp.inf, s)
    return jnp.concatenate(vals, axis=0), jnp.concatenate(picks, axis=0)


def _peer_select_body(h_ref, wq_ref, keys_ref, eid_ref, gate_ref, q_s, eid_t, gate_t):
    q_s[...] = jnp.dot(h_ref[...].astype(jnp.bfloat16), wq_ref[...], preferred_element_type=jnp.float32)

    def head(hd, carry):
        tops = []
        for p in range(2):
            col = pl.multiple_of((hd * 2 + p) * PEER_HALF, PEER_HALF)
            q_hp = q_s[:, pl.ds(col, PEER_HALF)].astype(jnp.bfloat16)
            s = lax.dot_general(keys_ref[hd * 2 + p], q_hp, (((1,), (1,)), ((), ())),
                                preferred_element_type=jnp.float32)
            tops.append(_top_rows(s, None, PEER_TOPK))
        (v0, i0), (v1, i1) = tops
        kf = float(PEER_TOPK)
        r16 = lax.broadcasted_iota(jnp.int32, v0.shape, 0).astype(jnp.float32)
        v0_rest = jnp.where(r16 >= float(SEL_NARROW), v0, -jnp.inf)
        cand = jnp.concatenate([v0[i:i + 1] + v1 for i in range(SEL_NARROW)]
                               + [v0_rest + v1[j:j + 1] for j in range(SEL_NARROW)], axis=0)
        rank = jnp.concatenate([i * kf + r16 for i in range(SEL_NARROW)]
                               + [jnp.where(r16 >= float(SEL_NARROW), r16 * kf + j, RANK_NONE)
                                  for j in range(SEL_NARROW)], axis=0)
        cid = jnp.concatenate([i0[i:i + 1] * float(N_KEYS) + i1 for i in range(SEL_NARROW)]
                              + [i0 * float(N_KEYS) + i1[j:j + 1] for j in range(SEL_NARROW)], axis=0)
        best, eid = _top_rows(cand, cid, PEER_TOPK, rank)
        ex = jnp.exp(best - best[0:1])
        row0 = pl.multiple_of(hd * PEER_TOPK, PEER_TOPK)
        eid_t[pl.ds(row0, PEER_TOPK), :] = eid
        gate_t[pl.ds(row0, PEER_TOPK), :] = ex / jnp.sum(ex, axis=0, keepdims=True)
        return carry

    lax.fori_loop(0, PEER_HEADS, head, 0)
    eid_ref[...] = eid_t[...].T.astype(jnp.int32)
    gate_ref[...] = gate_t[...].T


def _peer_select(h, wq, keys):
    t = h.shape[0]
    assert t % SEL_TB == 0
    qw = PEER_HEADS * PEER_KEY_DIM
    return pl.pallas_call(
        _peer_select_body,
        grid=(t // SEL_TB,),
        in_specs=[pl.BlockSpec((SEL_TB, D_MODEL), lambda i: (i, 0)),
                  pl.BlockSpec((D_MODEL, qw), lambda i: (0, 0)),
                  pl.BlockSpec((PEER_HEADS * 2, N_KEYS, PEER_HALF), lambda i: (0, 0, 0))],
        out_specs=[pl.BlockSpec((SEL_TB, N_PICK), lambda i: (i, 0)),
                   pl.BlockSpec((SEL_TB, N_PICK), lambda i: (i, 0))],
        out_shape=[jax.ShapeDtypeStruct((t, N_PICK), jnp.int32),
                   jax.ShapeDtypeStruct((t, N_PICK), jnp.float32)],
        scratch_shapes=[pltpu.VMEM((SEL_TB, qw), jnp.float32),
                        pltpu.VMEM((N_PICK, SEL_TB), jnp.float32),
                        pltpu.VMEM((N_PICK, SEL_TB), jnp.float32)],
        name="peer_select",
    )(h, wq.astype(jnp.bfloat16), keys.reshape(PEER_HEADS * 2, N_KEYS, PEER_HALF).astype(jnp.bfloat16))


def _peer(h, wq, keys, uv):
    shape = h.shape
    h2 = h.reshape(-1, shape[-1])
    eid, gate = _peer_select(h2, wq, keys)
    return _peer_experts(h2, eid, gate, uv).reshape(shape)


def kernel(x, c, ctx, c_ctx, w_mod, b_mod, ln_w, ln_b, ab_w_in, ab_gate_b, ab_norm_w, ab_sink, ab_w_out,
           gla_w_in, gla_gate_up, gla_gate_b, gla_norm_w, gla_w_out, peer_wq, peer_keys, peer_u, peer_v):
    for layer in range(DEPTH):
        last = layer == DEPTH - 1
        j = layer // 2
        mod_l = jax.nn.silu(c) @ w_mod[layer] + b_mod[layer]
        mod_c = jax.nn.silu(c_ctx) @ w_mod[layer] + b_mod[layer]
        sh1, sc1, g1, sh2, sc2, g2 = jnp.split(mod_l[:, None, :], 6, axis=-1)
        csh1, csc1, cg1, csh2, csc2, cg2 = jnp.split(mod_c, 6)
        hl = x * (1.0 + sc1) + sh1
        hc = ctx * (1.0 + csc1) + csh1
        if layer % 2 == 0:
            y_lat, y_ctx = _mixer_ab(hl, hc, ab_w_in[j], ab_gate_b[j], ab_norm_w[j], ab_sink[j], ab_w_out[j], not last)
        else:
            y_lat, y_ctx = _mixer_c(hl, hc, gla_w_in[j], gla_gate_up[j], gla_gate_b[j], gla_norm_w[j], gla_w_out[j],
                                    not last)
        x = _layernorm(DEEPNORM_ALPHA * x + g1 * y_lat, ln_w[layer, 0], ln_b[layer, 0])
        uv = jnp.concatenate([peer_u[layer].reshape(-1, UV_ROWS // 2, LANE),
                              peer_v[layer].reshape(-1, UV_ROWS // 2, LANE)], axis=1)
        f_lat = _peer(x * (1.0 + sc2) + sh2, peer_wq[layer], peer_keys[layer], uv)
        x = _layernorm(DEEPNORM_ALPHA * x + g2 * f_lat, ln_w[layer, 1], ln_b[layer, 1])
        if not last:
            ctx = _layernorm(DEEPNORM_ALPHA * ctx + cg1 * y_ctx, ln_w[layer, 0], ln_b[layer, 0])
            f_ctx = _peer(ctx * (1.0 + csc2) + csh2, peer_wq[layer], peer_keys[layer], uv)
            ctx = _layernorm(DEEPNORM_ALPHA * ctx + cg2 * f_ctx, ln_w[layer, 1], ln_b[layer, 1])
    return x
```
